```python
import math
import jax, jax.numpy as jnp
from jax import lax
import numpy as np

D_MODEL = 2048
BATCH = 8
SEQ = 4096
DEPTH = 4
DEC_BATCH = 4
DEC_SEQ = 4096
PAST_LEN = 128

N_MIXERS = 2
N_A_LAYERS = (DEPTH + 1) // 2
N_B_LAYERS = DEPTH // 2
EPS = 1e-6
ROPE_THETA = 10000.0
BLOCK = 128
A_HEADS = 16
A_KV_HEADS = 4
A_GROUP = A_HEADS // A_KV_HEADS
A_HEAD_DIM = D_MODEL // A_HEADS
WINDOW = 128
B_HEADS = 16
Q_LORA = 512
KV_LORA = 512
NOPE_DIM = 128
ROPE_DIM = 64
V_DIM = 128
D_FF = 5632
CONV_W = 3

kernel_name = "hybrid_swa_sink_mla_convffn_encoder"


def rmsnorm(x, g):
    xf = x.astype(jnp.float32)
    y = xf * lax.rsqrt(jnp.mean(xf * xf, axis=-1, keepdims=True) + EPS) * g.astype(jnp.float32)
    return y.astype(x.dtype)


def rope_tables(seq, dim):
    pos = jnp.arange(seq, dtype=jnp.float32)
    inv = 1.0 / (ROPE_THETA ** (jnp.arange(0, dim, 2, dtype=jnp.float32) / dim))
    ang = pos[:, None] * inv[None, :]
    return jnp.cos(ang), jnp.sin(ang)


def apply_rope(x, cos, sin):
    shp = (cos.shape[0],) + (1,) * (x.ndim - 3) + (cos.shape[1],)
    c = cos.reshape(shp)
    s = sin.reshape(shp)
    xf = x.astype(jnp.float32)
    x1, x2 = jnp.split(xf, 2, axis=-1)
    out = jnp.concatenate([x1 * c - x2 * s, x2 * c + x1 * s], axis=-1)
    return out.astype(x.dtype)


def window_gqa_sink(x, w_qkv, w_o, sink, cos, sin):
    B, S, _ = x.shape
    nb = S // BLOCK
    qkv = x @ w_qkv
    qd = A_HEADS * A_HEAD_DIM
    kd = A_KV_HEADS * A_HEAD_DIM
    q = qkv[..., :qd].reshape(B, S, A_KV_HEADS, A_GROUP, A_HEAD_DIM)
    k = qkv[..., qd:qd + kd].reshape(B, S, A_KV_HEADS, A_HEAD_DIM)
    v = qkv[..., qd + kd:].reshape(B, S, A_KV_HEADS, A_HEAD_DIM)
    q = apply_rope(q, cos, sin)
    k = apply_rope(k, cos, sin)
    pad = ((0, 0), (WINDOW, WINDOW), (0, 0), (0, 0))
    kp = jnp.pad(k, pad)
    vp = jnp.pad(v, pad)
    qb = jnp.moveaxis(q.reshape(B, nb, BLOCK, A_KV_HEADS, A_GROUP, A_HEAD_DIM), 1, 0)
    span = BLOCK + 2 * WINDOW
    scale = A_HEAD_DIM ** -0.5
    sink_b = sink.astype(jnp.float32).reshape(A_KV_HEADS, A_GROUP)[None, :, :, None, None]
    a_idx = jnp.arange(BLOCK)[:, None]
    c_idx = jnp.arange(span)[None, :]
    rel = c_idx - a_idx

    def block_fn(args):
        n, qn = args
        start = n * BLOCK
        kn = lax.dynamic_slice_in_dim(kp, start, span, axis=1)
        vn = lax.dynamic_slice_in_dim(vp, start, span, axis=1)
        s = jnp.einsum('bqkgd,bckd->bkgqc', qn, kn).astype(jnp.float32) * scale
        j = start + c_idx - WINDOW
        valid = (rel >= 0) & (rel <= 2 * WINDOW) & (j >= 0) & (j < S)
        s = jnp.where(valid[None, None, None], s, -1e30)
        m = jnp.maximum(jnp.max(s, axis=-1, keepdims=True), sink_b)
        p = jnp.exp(s - m)
        denom = jnp.sum(p, axis=-1, keepdims=True) + jnp.exp(sink_b - m)
        return jnp.einsum('bkgqc,bckd->bqkgd', (p / denom).astype(vn.dtype), vn)

    o = lax.map(block_fn, (jnp.arange(nb), qb))
    o = jnp.moveaxis(o, 0, 1).reshape(B, S, A_HEADS * A_HEAD_DIM)
    return o @ w_o


def mla(x, w_in, q_norm, w_q_up, kv_norm, w_kv_up, w_o, cos, sin):
    B, S, _ = x.shape
    nb = S // BLOCK
    h = x @ w_in
    cq = rmsnorm(h[..., :Q_LORA], q_norm)
    ckv = rmsnorm(h[..., Q_LORA:Q_LORA + KV_LORA], kv_norm)
    k_rope = apply_rope(h[..., Q_LORA + KV_LORA:], cos, sin)
    q = (cq @ w_q_up).reshape(B, S, B_HEADS, NOPE_DIM + ROPE_DIM)
    q_nope = q[..., :NOPE_DIM]
    q_rope = apply_rope(q[..., NOPE_DIM:], cos, sin)
    kv = (ckv @ w_kv_up).reshape(B, S, B_HEADS, NOPE_DIM + V_DIM)
    k_nope = kv[..., :NOPE_DIM]
    v = kv[..., NOPE_DIM:]
    scale = (NOPE_DIM + ROPE_DIM) ** -0.5
    qn_b = jnp.moveaxis(q_nope.reshape(B, nb, BLOCK, B_HEADS, NOPE_DIM), 1, 0)
    qr_b = jnp.moveaxis(q_rope.reshape(B, nb, BLOCK, B_HEADS, ROPE_DIM), 1, 0)

    def block_fn(args):
        qn, qr = args
        s = (jnp.einsum('bqhd,bkhd->bhqk', qn, k_nope).astype(jnp.float32)
             + jnp.einsum('bqhr,bkr->bhqk', qr, k_rope).astype(jnp.float32)) * scale
        p = jax.nn.softmax(s, axis=-1)
        return jnp.einsum('bhqk,bkhd->bqhd', p.astype(v.dtype), v)

    o = lax.map(block_fn, (qn_b, qr_b))
    o = jnp.moveaxis(o, 0, 1).reshape(B, S, B_HEADS * V_DIM)
    return o @ w_o


def conv_ffn(x, w_in, conv_w, conv_b, w_out):
    h = x @ w_in
    hp = jnp.pad(h, ((0, 0), (1, 1), (0, 0)))
    h = hp[:, :-2] * conv_w[0] + hp[:, 1:-1] * conv_w[1] + hp[:, 2:] * conv_w[2] + conv_b
    g = h[..., :D_FF]
    u = h[..., D_FF:]
    return (jax.nn.silu(g) * u) @ w_out


def trunk(x, norm_mix, norm_ffn, norm_final, a_w_qkv, a_w_o, a_sink,
          b_w_in, b_q_norm, b_w_q_up, b_kv_norm, b_w_kv_up, b_w_o,
          f_w_in, f_conv_w, f_conv_b, f_w_out):
    S = x.shape[1]
    cos_a, sin_a = rope_tables(S, A_HEAD_DIM)
    cos_b, sin_b = rope_tables(S, ROPE_DIM)
    for i in range(DEPTH):
        h = rmsnorm(x, norm_mix[i])
        j = i // N_MIXERS
        if i % N_MIXERS == 0:
            x = x + window_gqa_sink(h, a_w_qkv[j], a_w_o[j], a_sink[j], cos_a, sin_a)
        else:
            x = x + mla(h, b_w_in[j], b_q_norm[j], b_w_q_up[j], b_kv_norm[j],
                        b_w_kv_up[j], b_w_o[j], cos_b, sin_b)
        x = x + conv_ffn(rmsnorm(x, norm_ffn[i]), f_w_in[i], f_conv_w[i], f_conv_b[i], f_w_out[i])
    return rmsnorm(x, norm_final)


def setup_inputs(seed: int = 0) -> dict:
    key = jax.random.key(seed)
    ks = jax.random.split(key, 20)
    f32 = jnp.float32

    def w(k, shape, fan_in):
        return jax.random.normal(k, shape, f32) * (fan_in ** -0.5)

    def gain(k, shape):
        return 1.0 + 0.05 * jax.random.normal(k, shape, f32)

    qkv_out = (A_HEADS + 2 * A_KV_HEADS) * A_HEAD_DIM
    return {
        "x_prompt": jax.random.normal(ks[0], (BATCH, SEQ, D_MODEL), f32),
        "x_sample": jax.random.normal(ks[1], (DEC_BATCH, DEC_SEQ, D_MODEL), f32),
        "norm_mix": gain(ks[2], (DEPTH, D_MODEL)),
        "norm_ffn": gain(ks[3], (DEPTH, D_MODEL)),
        "norm_final": gain(ks[4], (D_MODEL,)),
        "a_w_qkv": w(ks[5], (N_A_LAYERS, D_MODEL, qkv_out), D_MODEL),
        "a_w_o": w(ks[6], (N_A_LAYERS, A_HEADS * A_HEAD_DIM, D_MODEL), A_HEADS * A_HEAD_DIM),
        "a_sink": 0.5 * jax.random.normal(ks[7], (N_A_LAYERS, A_HEADS), f32),
        "b_w_in": w(ks[8], (N_B_LAYERS, D_MODEL, Q_LORA + KV_LORA + ROPE_DIM), D_MODEL),
        "b_q_norm": gain(ks[9], (N_B_LAYERS, Q_LORA)),
        "b_w_q_up": w(ks[10], (N_B_LAYERS, Q_LORA, B_HEADS * (NOPE_DIM + ROPE_DIM)), Q_LORA),
        "b_kv_norm": gain(ks[11], (N_B_LAYERS, KV_LORA)),
        "b_w_kv_up": w(ks[12], (N_B_LAYERS, KV_LORA, B_HEADS * (NOPE_DIM + V_DIM)), KV_LORA),
        "b_w_o": w(ks[13], (N_B_LAYERS, B_HEADS * V_DIM, D_MODEL), B_HEADS * V_DIM),
        "f_w_in": w(ks[14], (DEPTH, D_MODEL, 2 * D_FF), D_MODEL),
        "f_conv_w": w(ks[15], (DEPTH, CONV_W, 2 * D_FF), CONV_W),
        "f_conv_b": 0.01 * jax.random.normal(ks[16], (DEPTH, 2 * D_FF), f32),
        "f_w_out": w(ks[17], (DEPTH, D_FF, D_MODEL), D_FF),
    }


def reference(x_prompt, x_sample, norm_mix, norm_ffn, norm_final, a_w_qkv, a_w_o, a_sink,
              b_w_in, b_q_norm, b_w_q_up, b_kv_norm, b_w_kv_up, b_w_o,
              f_w_in, f_conv_w, f_conv_b, f_w_out):
    y_prompt = trunk(x_prompt, norm_mix, norm_ffn, norm_final, a_w_qkv, a_w_o, a_sink,
                     b_w_in, b_q_norm, b_w_q_up, b_kv_norm, b_w_kv_up, b_w_o,
                     f_w_in, f_conv_w, f_conv_b, f_w_out)
    y_sample = trunk(x_sample, norm_mix, norm_ffn, norm_final, a_w_qkv, a_w_o, a_sink,
                     b_w_in, b_q_norm, b_w_q_up, b_kv_norm, b_w_kv_up, b_w_o,
                     f_w_in, f_conv_w, f_conv_b, f_w_out)
    return (y_prompt, y_sample)
```

```python
import functools
import math

import jax
import jax.numpy as jnp
from jax import lax
from jax.experimental import pallas as pl
from jax.experimental.pallas import tpu as pltpu

F32 = jnp.float32
BF16 = jnp.bfloat16

EPS = 1e-6
ROPE_THETA = 10000.0
LANES = 128
BF16_ROWS = 16
WINDOW = 128
A_HEADS = 16
A_KV_HEADS = 4
A_GROUP = A_HEADS // A_KV_HEADS
A_HEAD_DIM = 128
B_HEADS = 16
Q_LORA = 512
KV_LORA = 512
NOPE_DIM = 128
ROPE_DIM = 64
V_DIM = 128
LOG2E = math.log2(math.e)
NEG_BIG = -1e30
VMEM_LIMIT = 56 * 1024 * 1024


def _params(sem):
    return pltpu.CompilerParams(dimension_semantics=sem, vmem_limit_bytes=VMEM_LIMIT)


def _rms(x, g):
    return x * lax.rsqrt(jnp.mean(x * x, axis=-1, keepdims=True) + EPS) * g


def _rope64(x, c, s1, s2):
    return x * c + pltpu.roll(x, 96, 1) * s1 + pltpu.roll(x, 32, 1) * s2


def _qkv_a_kernel(x_ref, g_ref, w_ref, cos_ref, sin_ref, o_ref, xn_ref, *,
                  tn, n_q_tiles, n_rope_tiles, q_scale):
    j = pl.program_id(2)

    @pl.when(j == 0)
    def _():
        xn_ref[...] = _rms(x_ref[0], g_ref[...]).astype(BF16)

    acc = jnp.dot(xn_ref[...], w_ref[...], preferred_element_type=F32)
    mult = jnp.where(j < n_q_tiles, q_scale, 1.0).astype(F32)
    rope = j < n_rope_tiles
    c = jnp.where(rope, cos_ref[...], 1.0) * mult
    s = jnp.where(rope, sin_ref[...], 0.0) * mult
    for h in range(tn // LANES):
        ch = acc[:, h * LANES:(h + 1) * LANES]
        o_ref[0, :, h * LANES:(h + 1) * LANES] = (
            ch * c + pltpu.roll(ch, LANES // 2, 1) * s).astype(BF16)


def _qkv_a(x, g, w, cos2, sin2, *, t=1024, tn=512):
    b, s, d = x.shape
    n = w.shape[1]
    q_cols = A_HEADS * A_HEAD_DIM
    k_cols = A_KV_HEADS * A_HEAD_DIM
    kern = functools.partial(
        _qkv_a_kernel, tn=tn, n_q_tiles=q_cols // tn,
        n_rope_tiles=(q_cols + k_cols) // tn,
        q_scale=A_HEAD_DIM ** -0.5 * LOG2E)
    return pl.pallas_call(
        kern,
        grid=(b, s // t, n // tn),
        in_specs=[
            pl.BlockSpec((1, t, d), lambda bi, i, j: (bi, i, 0)),
            pl.BlockSpec((1, d), lambda bi, i, j: (0, 0)),
            pl.BlockSpec((d, tn), lambda bi, i, j: (0, j)),
            pl.BlockSpec((t, LANES), lambda bi, i, j: (i, 0)),
            pl.BlockSpec((t, LANES), lambda bi, i, j: (i, 0)),
        ],
        out_specs=pl.BlockSpec((1, t, tn), lambda bi, i, j: (bi, i, j)),
        out_shape=jax.ShapeDtypeStruct((b, s, n), BF16),
        scratch_shapes=[pltpu.VMEM((t, d), BF16)],
        compiler_params=_params(("parallel", "parallel", "arbitrary")),
        name="qkv_a",
    )(x, g, w, cos2, sin2)


def _win_attn_kernel(sink_ref, q_ref, km_ref, kp_ref, kx_ref, vm_ref, vp_ref, vx_ref,
                     o_ref, kbuf, vbuf, *, tq, n_tiles):
    i = pl.program_id(1)
    blk = WINDOW
    span = 3 * blk
    kbuf[0:blk, :] = kp_ref[0]
    kbuf[blk:blk + tq, :] = km_ref[0]
    kbuf[blk + tq:, :] = kx_ref[0]
    vbuf[0:blk, :] = vp_ref[0]
    vbuf[blk:blk + tq, :] = vm_ref[0]
    vbuf[blk + tq:, :] = vx_ref[0]

    rows = A_GROUP * blk
    a_idx = lax.broadcasted_iota(jnp.int32, (rows, span), 0) % blk
    c_idx = lax.broadcasted_iota(jnp.int32, (rows, span), 1)
    rel = c_idx - a_idx
    band = (rel >= 0) & (rel <= 2 * WINDOW)
    grp = lax.broadcasted_iota(jnp.int32, (rows, 1), 0) // blk
    lo = jnp.where(i == 0, blk, 0)
    hi = jnp.where(i == n_tiles - 1, 2 * blk, span)

    n_sub = tq // blk
    for r in range(n_sub):
        valid = band
        if r == 0:
            valid = valid & (c_idx >= lo)
        if r == n_sub - 1:
            valid = valid & (c_idx < hi)
        for kh in range(A_KV_HEADS):
            qg = jnp.concatenate(
                [q_ref[0, r * blk:(r + 1) * blk,
                       (kh * A_GROUP + g) * A_HEAD_DIM:(kh * A_GROUP + g + 1) * A_HEAD_DIM]
                 for g in range(A_GROUP)], axis=0)
            kw = kbuf[r * blk:r * blk + span, kh * A_HEAD_DIM:(kh + 1) * A_HEAD_DIM]
            vw = vbuf[r * blk:r * blk + span, kh * A_HEAD_DIM:(kh + 1) * A_HEAD_DIM]
            s = lax.dot_general(qg, kw, (((1,), (1,)), ((), ())),
                                preferred_element_type=F32)
            s = jnp.where(valid, s, NEG_BIG)
            sk = jnp.zeros((rows, 1), F32)
            for g in range(A_GROUP):
                sk = jnp.where(grp == g, sink_ref[kh * A_GROUP + g], sk)
            m = jnp.maximum(jnp.max(s, axis=1, keepdims=True), sk)
            p = jnp.exp2(s - m)
            den = jnp.sum(p, axis=1, keepdims=True) + jnp.exp2(sk - m)
            o = jnp.dot(p.astype(BF16), vw, preferred_element_type=F32) / den
            for g in range(A_GROUP):
                col = (kh * A_GROUP + g) * A_HEAD_DIM
                o_ref[0, r * blk:(r + 1) * blk, col:col + A_HEAD_DIM] = (
                    o[g * blk:(g + 1) * blk].astype(BF16))


def _win_attn(qkv, sink2, *, tq=512):
    b, s, _ = qkv.shape
    d = A_HEADS * A_HEAD_DIM
    kd = A_KV_HEADS * A_HEAD_DIM
    blk = WINDOW
    n_tiles = s // tq
    sub = tq // blk
    nb = s // blk
    k_col = d // kd
    v_col = k_col + 1

    def main(col):
        return pl.BlockSpec((1, tq, kd), lambda bi, i: (bi, i, col))

    def prev(col):
        return pl.BlockSpec((1, blk, kd), lambda bi, i: (bi, jnp.maximum(i * sub - 1, 0), col))

    def nxt(col):
        return pl.BlockSpec((1, blk, kd), lambda bi, i: (bi, jnp.minimum((i + 1) * sub, nb - 1), col))

    kern = functools.partial(_win_attn_kernel, tq=tq, n_tiles=n_tiles)
    return pl.pallas_call(
        kern,
        grid=(b, n_tiles),
        in_specs=[
            pl.BlockSpec(memory_space=pltpu.SMEM),
            pl.BlockSpec((1, tq, d), lambda bi, i: (bi, i, 0)),
            main(k_col), prev(k_col), nxt(k_col),
            main(v_col), prev(v_col), nxt(v_col),
        ],
        out_specs=pl.BlockSpec((1, tq, d), lambda bi, i: (bi, i, 0)),
        out_shape=jax.ShapeDtypeStruct((b, s, d), BF16),
        scratch_shapes=[pltpu.VMEM((tq + 2 * blk, kd), BF16),
                        pltpu.VMEM((tq + 2 * blk, kd), BF16)],
        compiler_params=_params(("parallel", "parallel")),
        name="win_attn",
    )(sink2, qkv, qkv, qkv, qkv, qkv, qkv, qkv)


def _in_b_kernel(x_ref, g_ref, w_ref, qn_ref, kvn_ref, c_ref, s1_ref, s2_ref,
                 cq_ref, ckv_ref, kr_ref):
    xn = _rms(x_ref[0], g_ref[...]).astype(BF16)
    h = jnp.dot(xn, w_ref[...], preferred_element_type=F32)
    cq_ref[0] = _rms(h[:, :Q_LORA], qn_ref[...]).astype(BF16)
    ckv_ref[0] = _rms(h[:, Q_LORA:Q_LORA + KV_LORA], kvn_ref[...]).astype(BF16)
    kr_ref[0] = _rope64(h[:, Q_LORA + KV_LORA:], c_ref[...], s1_ref[...],
                        s2_ref[...]).astype(BF16)


def _in_b(x, g, w, qn, kvn, c, s1, s2, *, t=512):
    b, s, d = x.shape
    n = w.shape[1]
    row = lambda bi, i: (bi, i, 0)
    fixed = lambda bi, i: (0, 0)
    tab = pl.BlockSpec((t, LANES), lambda bi, i: (i, 0))
    return pl.pallas_call(
        _in_b_kernel,
        grid=(b, s // t),
        in_specs=[
            pl.BlockSpec((1, t, d), row),
            pl.BlockSpec((1, d), fixed),
            pl.BlockSpec((d, n), fixed),
            pl.BlockSpec((1, Q_LORA), fixed),
            pl.BlockSpec((1, KV_LORA), fixed),
            tab, tab, tab,
        ],
        out_specs=[pl.BlockSpec((1, t, Q_LORA), row),
                   pl.BlockSpec((1, t, KV_LORA), row),
                   pl.BlockSpec((1, t, LANES), row)],
        out_shape=[jax.ShapeDtypeStruct((b, s, Q_LORA), BF16),
                   jax.ShapeDtypeStruct((b, s, KV_LORA), BF16),
                   jax.ShapeDtypeStruct((b, s, LANES), BF16)],
        compiler_params=_params(("parallel", "parallel")),
        name="in_b",
    )(x, g, w, qn, kvn, c, s1, s2)


def _proj_q_kernel(x_ref, w_ref, c_ref, s1_ref, s2_ref, o_ref, *, tn, scale):
    acc = jnp.dot(x_ref[0], w_ref[...], preferred_element_type=F32)
    c = c_ref[...] * scale
    s1 = s1_ref[...] * scale
    s2 = s2_ref[...] * scale
    hw = 2 * LANES
    for h in range(tn // hw):
        o_ref[0, :, h * hw:h * hw + LANES] = (
            acc[:, h * hw:h * hw + LANES] * scale).astype(BF16)
        o_ref[0, :, h * hw + LANES:(h + 1) * hw] = _rope64(
            acc[:, h * hw + LANES:(h + 1) * hw], c, s1, s2).astype(BF16)


def _proj_kernel(x_ref, w_ref, o_ref):
    o_ref[0] = jnp.dot(x_ref[0], w_ref[...], preferred_element_type=F32).astype(BF16)


def _proj(x, w, tables=None, *, t=1024, tn=1024):
    b, s, k = x.shape
    n = w.shape[1]
    in_specs = [pl.BlockSpec((1, t, k), lambda bi, i, j: (bi, i, 0)),
                pl.BlockSpec((k, tn), lambda bi, i, j: (0, j))]
    args = [x, w]
    if tables is None:
        kern = _proj_kernel
        name = "proj_kv"
    else:
        kern = functools.partial(_proj_q_kernel, tn=tn,
                                 scale=(NOPE_DIM + ROPE_DIM) ** -0.5 * LOG2E)
        in_specs += [pl.BlockSpec((t, LANES), lambda bi, i, j: (i, 0))] * 3
        args += list(tables)
        name = "proj_q"
    return pl.pallas_call(
        kern,
        grid=(b, s // t, n // tn),
        in_specs=in_specs,
        out_specs=pl.BlockSpec((1, t, tn), lambda bi, i, j: (bi, i, j)),
        out_shape=jax.ShapeDtypeStruct((b, s, n), BF16),
        compiler_params=_params(("parallel", "parallel", "parallel")),
        name=name,
    )(*args)


def _mla_kernel(q_ref, kn_ref, v_ref, kr_ref, o_ref, kcat_ref):
    i = pl.program_id(2)

    @pl.when(i == 0)
    def _():
        kcat_ref[:, :LANES] = kn_ref[0]
        kcat_ref[:, LANES:] = kr_ref[0]

    s = lax.dot_general(q_ref[0], kcat_ref[...], (((1,), (1,)), ((), ())),
                        preferred_element_type=F32)
    m = jnp.max(s, axis=1, keepdims=True)
    p = jnp.exp2(s - m)
    den = jnp.sum(p, axis=1, keepdims=True)
    o = jnp.dot(p.astype(BF16), v_ref[0], preferred_element_type=F32)
    o_ref[0] = (o / den).astype(BF16)


def _mla_attn(q, kv, kr, *, tq=256):
    b, s, _ = q.shape
    qw = 2 * LANES
    return pl.pallas_call(
        _mla_kernel,
        grid=(b, B_HEADS, s // tq),
        in_specs=[
            pl.BlockSpec((1, tq, qw), lambda bi, h, i: (bi, i, h)),
            pl.BlockSpec((1, s, NOPE_DIM), lambda bi, h, i: (bi, 0, h)),
            pl.BlockSpec((1, s, V_DIM), lambda bi, h, i: (bi, 0, B_HEADS + h)),
            pl.BlockSpec((1, s, LANES), lambda bi, h, i: (bi, 0, 0)),
        ],
        out_specs=pl.BlockSpec((1, tq, V_DIM), lambda bi, h, i: (bi, i, h)),
        out_shape=jax.ShapeDtypeStruct((b, s, B_HEADS * V_DIM), BF16),
        scratch_shapes=[pltpu.VMEM((s, qw), BF16)],
        compiler_params=_params(("parallel", "parallel", "arbitrary")),
        name="mla_attn",
    )(q, kv, kv, kr)


def _out_proj_kernel(o_ref, w_ref, x_ref, y_ref):
    y_ref[0] = x_ref[0] + jnp.dot(o_ref[0], w_ref[...], preferred_element_type=F32)


def _out_proj(o, w, x, *, t=1024, tn=512):
    b, s, k = o.shape
    n = w.shape[1]
    return pl.pallas_call(
        _out_proj_kernel,
        grid=(b, s // t, n // tn),
        in_specs=[pl.BlockSpec((1, t, k), lambda bi, i, j: (bi, i, 0)),
                  pl.BlockSpec((k, tn), lambda bi, i, j: (0, j)),
                  pl.BlockSpec((1, t, tn), lambda bi, i, j: (bi, i, j))],
        out_specs=pl.BlockSpec((1, t, tn), lambda bi, i, j: (bi, i, j)),
        out_shape=jax.ShapeDtypeStruct((b, s, n), F32),
        compiler_params=_params(("parallel", "parallel", "parallel")),
        name="out_proj",
    )(o, w, x)


def _ffn_kernel(xp_ref, x_ref, xx_ref, g_ref, wg_ref, wu_ref, cwg_ref, cwu_ref,
                cbg_ref, cbu_ref, wo_ref, y_ref, xn_ref, *, t, n_tiles):
    i = pl.program_id(1)
    f = pl.program_id(2)
    halo = BF16_ROWS

    @pl.when(f == 0)
    def _():
        g = g_ref[...]
        x = x_ref[0]
        prev = jnp.where(i > 0, _rms(xp_ref[0], g), 0.0)
        nxt = jnp.where(i < n_tiles - 1, _rms(xx_ref[0], g), 0.0)
        xn_ref[0:halo, :] = prev.astype(BF16)
        xn_ref[halo:halo + t, :] = _rms(x, g).astype(BF16)
        xn_ref[halo + t:, :] = nxt.astype(BF16)
        y_ref[0] = x

    xn = xn_ref[...]

    def conv(w_ref, cw_ref, cb_ref):
        h = jnp.dot(xn, w_ref[...], preferred_element_type=F32)
        cw = cw_ref[...]
        return (h[halo - 1:halo - 1 + t] * cw[0:1] + h[halo:halo + t] * cw[1:2]
                + h[halo + 1:halo + 1 + t] * cw[2:3] + cb_ref[...])

    gc = conv(wg_ref, cwg_ref, cbg_ref)
    uc = conv(wu_ref, cwu_ref, cbu_ref)
    a = (gc * jax.nn.sigmoid(gc) * uc).astype(BF16)
    y_ref[0] += jnp.dot(a, wo_ref[...], preferred_element_type=F32)


def _ffn(x, g, w_in, conv_w, conv_b, w_out, *, t=512, tf=512):
    b, s, d = x.shape
    d_ff = w_out.shape[0]
    nf = d_ff // tf
    n_tiles = s // t
    halo = BF16_ROWS
    hb = t // halo
    last_hb = s // halo - 1
    kern = functools.partial(_ffn_kernel, t=t, n_tiles=n_tiles)
    gate = lambda bi, i, f: (0, f)
    up = lambda bi, i, f: (0, nf + f)
    return pl.pallas_call(
        kern,
        grid=(b, n_tiles, nf),
        in_specs=[
            pl.BlockSpec((1, halo, d), lambda bi, i, f: (bi, jnp.maximum(i * hb - 1, 0), 0)),
            pl.BlockSpec((1, t, d), lambda bi, i, f: (bi, i, 0)),
            pl.BlockSpec((1, halo, d), lambda bi, i, f: (bi, jnp.minimum((i + 1) * hb, last_hb), 0)),
            pl.BlockSpec((1, d), lambda bi, i, f: (0, 0)),
            pl.BlockSpec((d, tf), gate),
            pl.BlockSpec((d, tf), up),
            pl.BlockSpec((3, tf), gate),
            pl.BlockSpec((3, tf), up),
            pl.BlockSpec((1, tf), gate),
            pl.BlockSpec((1, tf), up),
            pl.BlockSpec((tf, d), lambda bi, i, f: (f, 0)),
        ],
        out_specs=pl.BlockSpec((1, t, d), lambda bi, i, f: (bi, i, 0)),
        out_shape=jax.ShapeDtypeStruct((b, s, d), F32),
        scratch_shapes=[pltpu.VMEM((t + 2 * halo, d), BF16)],
        compiler_params=_params(("parallel", "parallel", "arbitrary")),
        name="ffn",
    )(x, x, x, g, w_in, w_in, conv_w, conv_w, conv_b, conv_b, w_out)


def _final_norm_kernel(x_ref, g_ref, y_ref):
    y_ref[0] = _rms(x_ref[0], g_ref[...])


def _final_norm(x, g, *, t=1024):
    b, s, d = x.shape
    return pl.pallas_call(
        _final_norm_kernel,
        grid=(b, s // t),
        in_specs=[pl.BlockSpec((1, t, d), lambda bi, i: (bi, i, 0)),
                  pl.BlockSpec((1, d), lambda bi, i: (0, 0))],
        out_specs=pl.BlockSpec((1, t, d), lambda bi, i: (bi, i, 0)),
        out_shape=jax.ShapeDtypeStruct((b, s, d), F32),
        compiler_params=_params(("parallel", "parallel")),
        name="final_norm",
    )(x, g)


def _rope_tables(seq, dim):
    pos = jnp.arange(seq, dtype=F32)
    inv = 1.0 / (ROPE_THETA ** (jnp.arange(0, dim, 2, dtype=F32) / dim))
    ang = pos[:, None] * inv[None, :]
    return jnp.cos(ang), jnp.sin(ang)


def _prep_tables(seq):
    cos_a, sin_a = _rope_tables(seq, A_HEAD_DIM)
    cos2 = jnp.concatenate([cos_a, cos_a], axis=1)
    sin2 = jnp.concatenate([-sin_a, sin_a], axis=1)
    cos_b, sin_b = _rope_tables(seq, ROPE_DIM)
    half = ROPE_DIM // 2
    z = jnp.zeros((seq, half), F32)
    zz = jnp.zeros((seq, LANES - ROPE_DIM), F32)
    c = jnp.concatenate([cos_b, cos_b, zz], axis=1)
    s1 = jnp.concatenate([-sin_b, z, zz], axis=1)
    s2 = jnp.concatenate([z, sin_b, zz], axis=1)
    return (cos2, sin2), (c, s1, s2)


def _prep_b_weights(w_in, w_q_up, w_kv_up):
    pad = jnp.zeros((w_in.shape[0], LANES - ROPE_DIM), w_in.dtype)
    w_in_p = jnp.concatenate([w_in, pad], axis=1).astype(BF16)
    k = w_q_up.shape[0]
    wq = w_q_up.reshape(k, B_HEADS, NOPE_DIM + ROPE_DIM)
    wq = jnp.concatenate(
        [wq, jnp.zeros((k, B_HEADS, 2 * LANES - NOPE_DIM - ROPE_DIM), wq.dtype)], axis=2)
    wq = wq.reshape(k, B_HEADS * 2 * LANES).astype(BF16)
    wkv = w_kv_up.reshape(k, B_HEADS, NOPE_DIM + V_DIM)
    wkv = jnp.concatenate(
        [wkv[:, :, :NOPE_DIM].reshape(k, B_HEADS * NOPE_DIM),
         wkv[:, :, NOPE_DIM:].reshape(k, B_HEADS * V_DIM)], axis=1).astype(BF16)
    return w_in_p, wq, wkv


def _trunk(x, p, tabs_a, tabs_b):
    depth = p["norm_mix"].shape[0]
    for i in range(depth):
        j = i // 2
        g_mix = p["norm_mix"][i][None, :]
        if i % 2 == 0:
            qkv = _qkv_a(x, g_mix, p["a_w_qkv"][j], *tabs_a)
            o = _win_attn(qkv, p["a_sink2"][j])
            x = _out_proj(o, p["a_w_o"][j], x)
        else:
            w_in_p, wq, wkv = p["b_w"][j]
            cq, ckv, kr = _in_b(x, g_mix, w_in_p, p["b_q_norm"][j][None, :],
                                p["b_kv_norm"][j][None, :], *tabs_b)
            q = _proj(cq, wq, tabs_b)
            kv = _proj(ckv, wkv)
            o = _mla_attn(q, kv, kr)
            x = _out_proj(o, p["b_w_o"][j], x)
        x = _ffn(x, p["norm_ffn"][i][None, :], p["f_w_in"][i], p["f_conv_w"][i],
                 p["f_conv_b"][i][None, :], p["f_w_out"][i])
    return _final_norm(x, p["norm_final"][None, :])


def kernel(x_prompt, x_sample, norm_mix, norm_ffn, norm_final, a_w_qkv, a_w_o, a_sink,
           b_w_in, b_q_norm, b_w_q_up, b_kv_norm, b_w_kv_up, b_w_o,
           f_w_in, f_conv_w, f_conv_b, f_w_out):
    p = {
        "norm_mix": norm_mix, "norm_ffn": norm_ffn, "norm_final": norm_final,
        "a_w_qkv": a_w_qkv.astype(BF16), "a_w_o": a_w_o.astype(BF16),
        "a_sink2": a_sink * LOG2E,
        "b_q_norm": b_q_norm, "b_kv_norm": b_kv_norm, "b_w_o": b_w_o.astype(BF16),
        "b_w": [_prep_b_weights(b_w_in[j], b_w_q_up[j], b_w_kv_up[j])
                for j in range(b_w_in.shape[0])],
        "f_w_in": f_w_in.astype(BF16), "f_conv_w": f_conv_w, "f_conv_b": f_conv_b,
        "f_w_out": f_w_out.astype(BF16),
    }
    outs = []
    for x in (x_prompt, x_sample):
        tabs_a, tabs_b = _prep_tables(x.shape[1])
        outs.append(_trunk(x, p, tabs_a, tabs_b))
    return tuple(outs)
```

```python
import functools
import math

import jax
import jax.numpy as jnp
from jax import lax
from jax.experimental import pallas as pl
from jax.experimental.pallas import tpu as pltpu

F32 = jnp.float32
BF16 = jnp.bfloat16

EPS = 1e-6
ROPE_THETA = 10000.0
LANES = 128
BF16_ROWS = 16
WINDOW = 128
A_HEADS = 16
A_KV_HEADS = 4
A_GROUP = A_HEADS // A_KV_HEADS
A_HEAD_DIM = 128
B_HEADS = 16
Q_LORA = 512
KV_LORA = 512
NOPE_DIM = 128
ROPE_DIM = 64
V_DIM = 128
LOG2E = math.log2(math.e)
NEG_BIG = -1e30
VMEM_LIMIT = 56 * 1024 * 1024


def _params(sem):
    return pltpu.CompilerParams(dimension_semantics=sem, vmem_limit_bytes=VMEM_LIMIT)


def _rms(x, g):
    return x * lax.rsqrt(jnp.mean(x * x, axis=-1, keepdims=True) + EPS) * g


def _rope64(x, c, s1, s2):
    return x * c + pltpu.roll(x, 96, 1) * s1 + pltpu.roll(x, 32, 1) * s2


def _qkv_a_kernel(x_ref, g_ref, w_ref, cos_ref, sin_ref, o_ref, xn_ref, *,
                  tn, n_q_tiles, n_rope_tiles, q_scale):
    j = pl.program_id(2)

    @pl.when(j == 0)
    def _():
        xn_ref[...] = _rms(x_ref[0], g_ref[...]).astype(BF16)

    acc = jnp.dot(xn_ref[...], w_ref[...], preferred_element_type=F32)
    mult = jnp.where(j < n_q_tiles, q_scale, 1.0).astype(F32)
    rope = j < n_rope_tiles
    c = jnp.where(rope, cos_ref[...], 1.0) * mult
    s = jnp.where(rope, sin_ref[...], 0.0) * mult
    for h in range(tn // LANES):
        ch = acc[:, h * LANES:(h + 1) * LANES]
        o_ref[0, :, h * LANES:(h + 1) * LANES] = (
            ch * c + pltpu.roll(ch, LANES // 2, 1) * s).astype(BF16)


def _qkv_a(x, g, w, cos2, sin2, *, t=1024, tn=512):
    b, s, d = x.shape
    n = w.shape[1]
    q_cols = A_HEADS * A_HEAD_DIM
    k_cols = A_KV_HEADS * A_HEAD_DIM
    kern = functools.partial(
        _qkv_a_kernel, tn=tn, n_q_tiles=q_cols // tn,
        n_rope_tiles=(q_cols + k_cols) // tn,
        q_scale=A_HEAD_DIM ** -0.5 * LOG2E)
    return pl.pallas_call(
        kern,
        grid=(b, s // t, n // tn),
        in_specs=[
            pl.BlockSpec((1, t, d), lambda bi, i, j: (bi, i, 0)),
            pl.BlockSpec((1, d), lambda bi, i, j: (0, 0)),
            pl.BlockSpec((d, tn), lambda bi, i, j: (0, j)),
            pl.BlockSpec((t, LANES), lambda bi, i, j: (i, 0)),
            pl.BlockSpec((t, LANES), lambda bi, i, j: (i, 0)),
        ],
        out_specs=pl.BlockSpec((1, t, tn), lambda bi, i, j: (bi, i, j)),
        out_shape=jax.ShapeDtypeStruct((b, s, n), BF16),
        scratch_shapes=[pltpu.VMEM((t, d), BF16)],
        compiler_params=_params(("parallel", "parallel", "arbitrary")),
        name="qkv_a",
    )(x, g, w, cos2, sin2)


def _win_attn_kernel(sink_ref, q_ref, km_ref, kp_ref, kx_ref, vm_ref, vp_ref, vx_ref,
                     o_ref, kbuf, vbuf, *, tq, n_tiles):
    i = pl.program_id(1)
    blk = WINDOW
    span = 3 * blk
    kbuf[0:blk, :] = kp_ref[0]
    kbuf[blk:blk + tq, :] = km_ref[0]
    kbuf[blk + tq:, :] = kx_ref[0]
    vbuf[0:blk, :] = vp_ref[0]
    vbuf[blk:blk + tq, :] = vm_ref[0]
    vbuf[blk + tq:, :] = vx_ref[0]

    rows = A_GROUP * blk
    a_idx = lax.broadcasted_iota(jnp.int32, (rows, span), 0) % blk
    c_idx = lax.broadcasted_iota(jnp.int32, (rows, span), 1)
    rel = c_idx - a_idx
    band = (rel >= 0) & (rel <= 2 * WINDOW)
    grp = lax.broadcasted_iota(jnp.int32, (rows, 1), 0) // blk
    lo = jnp.where(i == 0, blk, 0)
    hi = jnp.where(i == n_tiles - 1, 2 * blk, span)

    n_sub = tq // blk
    for r in range(n_sub):
        valid = band
        if r == 0:
            valid = valid & (c_idx >= lo)
        if r == n_sub - 1:
            valid = valid & (c_idx < hi)
        for kh in range(A_KV_HEADS):
            qg = jnp.concatenate(
                [q_ref[0, r * blk:(r + 1) * blk,
                       (kh * A_GROUP + g) * A_HEAD_DIM:(kh * A_GROUP + g + 1) * A_HEAD_DIM]
                 for g in range(A_GROUP)], axis=0)
            kw = kbuf[r * blk:r * blk + span, kh * A_HEAD_DIM:(kh + 1) * A_HEAD_DIM]
            vw = vbuf[r * blk:r * blk + span, kh * A_HEAD_DIM:(kh + 1) * A_HEAD_DIM]
            s = lax.dot_general(qg, kw, (((1,), (1,)), ((), ())),
                                preferred_element_type=F32)
            s = jnp.where(valid, s, NEG_BIG)
            sk = jnp.zeros((rows, 1), F32)
            for g in range(A_GROUP):
                sk = jnp.where(grp == g, sink_ref[kh * A_GROUP + g], sk)
            m = jnp.maximum(jnp.max(s, axis=1, keepdims=True), sk)
            p = jnp.exp2(s - m)
            den = jnp.sum(p, axis=1, keepdims=True) + jnp.exp2(sk - m)
            o = jnp.dot(p.astype(BF16), vw, preferred_element_type=F32) / den
            for g in range(A_GROUP):
                col = (kh * A_GROUP + g) * A_HEAD_DIM
                o_ref[0, r * blk:(r + 1) * blk, col:col + A_HEAD_DIM] = (
                    o[g * blk:(g + 1) * blk].astype(BF16))


def _win_attn(qkv, sink2, *, tq=512):
    b, s, _ = qkv.shape
    d = A_HEADS * A_HEAD_DIM
    kd = A_KV_HEADS * A_HEAD_DIM
    blk = WINDOW
    n_tiles = s // tq
    sub = tq // blk
    nb = s // blk
    k_col = d // kd
    v_col = k_col + 1

    def main(col):
        return pl.BlockSpec((1, tq, kd), lambda bi, i: (bi, i, col))

    def prev(col):
        return pl.BlockSpec((1, blk, kd), lambda bi, i: (bi, jnp.maximum(i * sub - 1, 0), col))

    def nxt(col):
        return pl.BlockSpec((1, blk, kd), lambda bi, i: (bi, jnp.minimum((i + 1) * sub, nb - 1), col))

    kern = functools.partial(_win_attn_kernel, tq=tq, n_tiles=n_tiles)
    return pl.pallas_call(
        kern,
        grid=(b, n_tiles),
        in_specs=[
            pl.BlockSpec(memory_space=pltpu.SMEM),
            pl.BlockSpec((1, tq, d), lambda bi, i: (bi, i, 0)),
            main(k_col), prev(k_col), nxt(k_col),
            main(v_col), prev(v_col), nxt(v_col),
        ],
        out_specs=pl.BlockSpec((1, tq, d), lambda bi, i: (bi, i, 0)),
        out_shape=jax.ShapeDtypeStruct((b, s, d), BF16),
        scratch_shapes=[pltpu.VMEM((tq + 2 * blk, kd), BF16),
                        pltpu.VMEM((tq + 2 * blk, kd), BF16)],
        compiler_params=_params(("parallel", "parallel")),
        name="win_attn",
    )(sink2, qkv, qkv, qkv, qkv, qkv, qkv, qkv)


def _in_b_kernel(x_ref, g_ref, w_ref, qn_ref, kvn_ref, c_ref, s1_ref, s2_ref,
                 cq_ref, ckv_ref, kr_ref):
    xn = _rms(x_ref[0], g_ref[...]).astype(BF16)
    h = jnp.dot(xn, w_ref[...], preferred_element_type=F32)
    cq_ref[0] = _rms(h[:, :Q_LORA], qn_ref[...]).astype(BF16)
    ckv_ref[0] = _rms(h[:, Q_LORA:Q_LORA + KV_LORA], kvn_ref[...]).astype(BF16)
    kr_ref[0] = _rope64(h[:, Q_LORA + KV_LORA:], c_ref[...], s1_ref[...],
                        s2_ref[...]).astype(BF16)


def _in_b(x, g, w, qn, kvn, c, s1, s2, *, t=512):
    b, s, d = x.shape
    n = w.shape[1]
    row = lambda bi, i: (bi, i, 0)
    fixed = lambda bi, i: (0, 0)
    tab = pl.BlockSpec((t, LANES), lambda bi, i: (i, 0))
    return pl.pallas_call(
        _in_b_kernel,
        grid=(b, s // t),
        in_specs=[
            pl.BlockSpec((1, t, d), row),
            pl.BlockSpec((1, d), fixed),
            pl.BlockSpec((d, n), fixed),
            pl.BlockSpec((1, Q_LORA), fixed),
            pl.BlockSpec((1, KV_LORA), fixed),
            tab, tab, tab,
        ],
        out_specs=[pl.BlockSpec((1, t, Q_LORA), row),
                   pl.BlockSpec((1, t, KV_LORA), row),
                   pl.BlockSpec((1, t, LANES), row)],
        out_shape=[jax.ShapeDtypeStruct((b, s, Q_LORA), BF16),
                   jax.ShapeDtypeStruct((b, s, KV_LORA), BF16),
                   jax.ShapeDtypeStruct((b, s, LANES), BF16)],
        compiler_params=_params(("parallel", "parallel")),
        name="in_b",
    )(x, g, w, qn, kvn, c, s1, s2)


def _proj_q_kernel(x_ref, w_ref, c_ref, s1_ref, s2_ref, o_ref, *, tn, scale):
    acc = jnp.dot(x_ref[0], w_ref[...], preferred_element_type=F32)
    c = c_ref[...] * scale
    s1 = s1_ref[...] * scale
    s2 = s2_ref[...] * scale
    hw = 2 * LANES
    for h in range(tn // hw):
        o_ref[0, :, h * hw:h * hw + LANES] = (
            acc[:, h * hw:h * hw + LANES] * scale).astype(BF16)
        o_ref[0, :, h * hw + LANES:(h + 1) * hw] = _rope64(
            acc[:, h * hw + LANES:(h + 1) * hw], c, s1, s2).astype(BF16)


def _proj_q(x, w, tables, *, t=1024, tn=1024):
    b, s, k = x.shape
    n = w.shape[1]
    kern = functools.partial(_proj_q_kernel, tn=tn,
                             scale=(NOPE_DIM + ROPE_DIM) ** -0.5 * LOG2E)
    tab = pl.BlockSpec((t, LANES), lambda bi, i, j: (i, 0))
    return pl.pallas_call(
        kern,
        grid=(b, s // t, n // tn),
        in_specs=[pl.BlockSpec((1, t, k), lambda bi, i, j: (bi, i, 0)),
                  pl.BlockSpec((k, tn), lambda bi, i, j: (0, j)),
                  tab, tab, tab],
        out_specs=pl.BlockSpec((1, t, tn), lambda bi, i, j: (bi, i, j)),
        out_shape=jax.ShapeDtypeStruct((b, s, n), BF16),
        compiler_params=_params(("parallel", "parallel", "parallel")),
        name="proj_q",
    )(x, w, *tables)


def _proj_kv_kernel(x_ref, wk_ref, wvt_ref, kn_ref, vt_ref):
    x = x_ref[0]
    kn_ref[0] = jnp.dot(x, wk_ref[...], preferred_element_type=F32).astype(BF16)
    vt_ref[0] = lax.dot_general(wvt_ref[...], x, (((1,), (1,)), ((), ())),
                                preferred_element_type=F32).astype(BF16)


def _proj_kv(x, wk, wvt, *, t=1024, tn=512):
    b, s, k = x.shape
    n = wk.shape[1]
    return pl.pallas_call(
        _proj_kv_kernel,
        grid=(b, s // t, n // tn),
        in_specs=[pl.BlockSpec((1, t, k), lambda bi, i, j: (bi, i, 0)),
                  pl.BlockSpec((k, tn), lambda bi, i, j: (0, j)),
                  pl.BlockSpec((tn, k), lambda bi, i, j: (j, 0))],
        out_specs=[pl.BlockSpec((1, t, tn), lambda bi, i, j: (bi, i, j)),
                   pl.BlockSpec((1, tn, t), lambda bi, i, j: (bi, j, i))],
        out_shape=[jax.ShapeDtypeStruct((b, s, n), BF16),
                   jax.ShapeDtypeStruct((b, n, s), BF16)],
        compiler_params=_params(("parallel", "parallel", "parallel")),
        name="proj_kv",
    )(x, wk, wvt)


def _mla_kernel(q_ref, kn_ref, kr_ref, vt_ref, o_ref, kcat_ref, st_ref, m_ref, *,
                n_i, kc, n_steps):
    t = pl.program_id(0)
    s_len, tq = st_ref.shape
    sub8 = 8

    @pl.when(t == 0)
    def _():
        st_ref[...] = jnp.zeros(st_ref.shape, F32)
        m_ref[...] = jnp.zeros(m_ref.shape, F32)

    @pl.when((t % n_i == 0) & (t < n_steps))
    def _():
        kcat_ref[:, :LANES] = kn_ref[0]
        kcat_ref[:, LANES:] = kr_ref[0]

    q = q_ref[0]
    m_old = m_ref[...]
    m_new = None
    den8 = None
    ot = None
    for c in range(s_len // kc):
        rows = slice(c * kc, (c + 1) * kc)
        p = jnp.exp2(st_ref[rows, :] - m_old)
        part = jnp.sum(p.reshape(kc // sub8, sub8, tq), axis=0)
        den8 = part if den8 is None else den8 + part
        pv = jnp.dot(vt_ref[0, :, rows], p.astype(BF16), preferred_element_type=F32)
        ot = pv if ot is None else ot + pv
        s_new = lax.dot_general(kcat_ref[rows, :], q, (((1,), (1,)), ((), ())),
                                preferred_element_type=F32)
        st_ref[rows, :] = s_new
        part = jnp.max(s_new.reshape(kc // sub8, sub8, tq), axis=0)
        m_new = part if m_new is None else jnp.maximum(m_new, part)
    m_ref[...] = jnp.max(m_new, axis=0, keepdims=True)
    den = jnp.sum(den8, axis=0, keepdims=True)
    o_ref[0] = (ot / den).T.astype(BF16)


def _mla_attn(q, kn, kr, vt, *, tq=512, kc=512):
    b, s, _ = q.shape
    qw = 2 * LANES
    n_i = s // tq
    n_steps = b * B_HEADS * n_i

    def unflat(t):
        bh = t // n_i
        return bh // B_HEADS, bh % B_HEADS, t % n_i

    def cur(t):
        return unflat(jnp.minimum(t, n_steps - 1))

    def prev(t):
        return unflat(jnp.maximum(t - 1, 0))

    def q_map(t):
        bi, h, i = cur(t)
        return bi, i, h

    def kn_map(t):
        bi, h, _ = cur(t)
        return bi, 0, h

    def kr_map(t):
        bi, _, _ = cur(t)
        return bi, 0, 0

    def vt_map(t):
        bi, h, _ = prev(t)
        return bi, h, 0

    def o_map(t):
        bi, h, i = prev(t)
        return bi, i, h

    kern = functools.partial(_mla_kernel, n_i=n_i, kc=kc, n_steps=n_steps)
    return pl.pallas_call(
        kern,
        grid=(n_steps + 1,),
        in_specs=[
            pl.BlockSpec((1, tq, qw), q_map),
            pl.BlockSpec((1, s, NOPE_DIM), kn_map),
            pl.BlockSpec((1, s, LANES), kr_map),
            pl.BlockSpec((1, V_DIM, s), vt_map),
        ],
        out_specs=pl.BlockSpec((1, tq, V_DIM), o_map),
        out_shape=jax.ShapeDtypeStruct((b, s, B_HEADS * V_DIM), BF16),
        scratch_shapes=[pltpu.VMEM((s, qw), BF16),
                        pltpu.VMEM((s, tq), F32),
                        pltpu.VMEM((1, tq), F32)],
        compiler_params=_params(("arbitrary",)),
        name="mla_attn",
    )(q, kn, kr, vt)


def _out_proj_kernel(o_ref, w_ref, x_ref, y_ref):
    y_ref[0] = x_ref[0] + jnp.dot(o_ref[0], w_ref[...], preferred_element_type=F32)


def _out_proj(o, w, x, *, t=1024, tn=512):
    b, s, k = o.shape
    n = w.shape[1]
    return pl.pallas_call(
        _out_proj_kernel,
        grid=(b, s // t, n // tn),
        in_specs=[pl.BlockSpec((1, t, k), lambda bi, i, j: (bi, i, 0)),
                  pl.BlockSpec((k, tn), lambda bi, i, j: (0, j)),
                  pl.BlockSpec((1, t, tn), lambda bi, i, j: (bi, i, j))],
        out_specs=pl.BlockSpec((1, t, tn), lambda bi, i, j: (bi, i, j)),
        out_shape=jax.ShapeDtypeStruct((b, s, n), F32),
        compiler_params=_params(("parallel", "parallel", "parallel")),
        name="out_proj",
    )(o, w, x)


def _ffn_kernel(xp_ref, x_ref, xx_ref, g_ref, wg_ref, wu_ref, cwg_ref, cwu_ref,
                cbg_ref, cbu_ref, wo_ref, y_ref, xn_ref, a0_ref, a1_ref, hg0_ref, hg1_ref,
                hu0_ref, hu1_ref, *, t, n_tiles, nf, n_steps):
    k = pl.program_id(0)
    f = k % nf
    i = (k // nf) % n_tiles
    halo = BF16_ROWS

    a_refs = (a0_ref, a1_ref)
    hg_refs = (hg0_ref, hg1_ref)
    hu_refs = (hu0_ref, hu1_ref)

    @pl.when(k == 0)
    def _():
        for ref in a_refs + hg_refs + hu_refs + (y_ref,):
            ref[...] = jnp.zeros(ref.shape, ref.dtype)

    @pl.when((f == 0) & (k < n_steps))
    def _():
        g = g_ref[...]
        prev = jnp.where(i > 0, _rms(xp_ref[0], g), 0.0)
        nxt = jnp.where(i < n_tiles - 1, _rms(xx_ref[0], g), 0.0)
        xn_ref[0:halo, :] = prev.astype(BF16)
        xn_ref[halo:halo + t, :] = _rms(x_ref[0], g).astype(BF16)
        xn_ref[halo + t:, :] = nxt.astype(BF16)

    @pl.when(f == 2)
    def _():
        y_ref[0] = x_ref[0]

    tf = a0_ref.shape[1]
    d = y_ref.shape[2]
    n_down = 4
    rb = 64
    blocks = [(r, c) for c in range(tf // LANES) for r in range(t // rb)]

    def conv(h_ref, r, cols, cw_ref, cb_ref):
        cw = cw_ref[:, cols]
        lo = halo + r * rb
        return (h_ref[lo - 1:lo - 1 + rb, cols] * cw[0:1]
                + h_ref[lo:lo + rb, cols] * cw[1:2]
                + h_ref[lo + 1:lo + 1 + rb, cols] * cw[2:3] + cb_ref[:, cols])

    def stages(p):
        def gate_block(r, c):
            cols = slice(c * LANES, (c + 1) * LANES)
            gc = conv(hg_refs[1 - p], r, cols, cwg_ref, cbg_ref)
            uc = conv(hu_refs[1 - p], r, cols, cwu_ref, cbu_ref)
            a_refs[1 - p][r * rb:(r + 1) * rb, cols] = (
                gc * jax.nn.sigmoid(gc) * uc).astype(BF16)

        def up_piece(h_ref, w_ref, lo, hi):
            h_ref[lo:hi, :] = jnp.dot(xn_ref[lo:hi, :], w_ref[...],
                                      preferred_element_type=F32)

        def down_piece(q):
            cols = slice(q * (d // n_down), (q + 1) * (d // n_down))
            y_ref[0, :, cols] += jnp.dot(a_refs[p][...], wo_ref[:, cols],
                                         preferred_element_type=F32)

        m_rows = t + 2 * halo
        m_half = m_rows // 2
        pieces = [(functools.partial(up_piece, h_ref, w_ref, lo, hi), 5)
                  for h_ref, w_ref in ((hg_refs[p], wg_ref), (hu_refs[p], wu_ref))
                  for lo, hi in ((0, m_half), (m_half, m_rows))]
        pieces += [(functools.partial(down_piece, q), 2) for q in range(n_down)]
        total = sum(w for _, w in pieces)
        done = 0
        acc = 0
        for piece, w in pieces:
            piece()
            acc += w
            upto = len(blocks) * acc // total
            for r, c in blocks[done:upto]:
                gate_block(r, c)
            done = upto

    @pl.when(k % 2 == 0)
    def _():
        stages(0)

    @pl.when(k % 2 == 1)
    def _():
        stages(1)


def _ffn(x, g, w_in, conv_w, conv_b, w_out, *, t=512, tf=512):
    b, s, d = x.shape
    d_ff = w_out.shape[0]
    nf = d_ff // tf
    assert nf >= 3
    n_tiles = s // t
    n_steps = b * n_tiles * nf
    halo = BF16_ROWS
    hb = t // halo
    last_hb = s // halo - 1

    def item(st, lag):
        st = jnp.clip(st - lag, 0, n_steps - 1)
        tile = st // nf
        return tile // n_tiles, tile % n_tiles, st % nf

    def x_map(st):
        bi, i, _ = item(st, 0)
        return bi, i, 0

    def xp_map(st):
        bi, i, _ = item(st, 0)
        return bi, jnp.maximum(i * hb - 1, 0), 0

    def xx_map(st):
        bi, i, _ = item(st, 0)
        return bi, jnp.minimum((i + 1) * hb, last_hb), 0

    def gate(lag):
        return lambda st: (0, item(st, lag)[2])

    def up(lag):
        return lambda st: (0, nf + item(st, lag)[2])

    def wo_map(st):
        return item(st, 2)[2], 0

    def y_map(st):
        bi, i, _ = item(st, 2)
        return bi, i, 0

    kern = functools.partial(_ffn_kernel, t=t, n_tiles=n_tiles, nf=nf, n_steps=n_steps)
    return pl.pallas_call(
        kern,
        grid=(n_steps + 2,),
        in_specs=[
            pl.BlockSpec((1, halo, d), xp_map),
            pl.BlockSpec((1, t, d), x_map),
            pl.BlockSpec((1, halo, d), xx_map),
            pl.BlockSpec((1, d), lambda st: (0, 0)),
            pl.BlockSpec((d, tf), gate(0)),
            pl.BlockSpec((d, tf), up(0)),
            pl.BlockSpec((3, tf), gate(1)),
            pl.BlockSpec((3, tf), up(1)),
            pl.BlockSpec((1, tf), gate(1)),
            pl.BlockSpec((1, tf), up(1)),
            pl.BlockSpec((tf, d), wo_map),
        ],
        out_specs=pl.BlockSpec((1, t, d), y_map),
        out_shape=jax.ShapeDtypeStruct((b, s, d), F32),
        scratch_shapes=[pltpu.VMEM((t + 2 * halo, d), BF16)]
        + [pltpu.VMEM((t, tf), BF16)] * 2
        + [pltpu.VMEM((t + 2 * halo, tf), F32)] * 4,
        compiler_params=_params(("arbitrary",)),
        name="ffn",
    )(x, x, x, g, w_in, w_in, conv_w, conv_w, conv_b, conv_b, w_out)


def _final_norm_kernel(x_ref, g_ref, y_ref):
    y_ref[0] = _rms(x_ref[0], g_ref[...])


def _final_norm(x, g, *, t=1024):
    b, s, d = x.shape
    return pl.pallas_call(
        _final_norm_kernel,
        grid=(b, s // t),
        in_specs=[pl.BlockSpec((1, t, d), lambda bi, i: (bi, i, 0)),
                  pl.BlockSpec((1, d), lambda bi, i: (0, 0))],
        out_specs=pl.BlockSpec((1, t, d), lambda bi, i: (bi, i, 0)),
        out_shape=jax.ShapeDtypeStruct((b, s, d), F32),
        compiler_params=_params(("parallel", "parallel")),
        name="final_norm",
    )(x, g)


def _rope_tables(seq, dim):
    pos = jnp.arange(seq, dtype=F32)
    inv = 1.0 / (ROPE_THETA ** (jnp.arange(0, dim, 2, dtype=F32) / dim))
    ang = pos[:, None] * inv[None, :]
    return jnp.cos(ang), jnp.sin(ang)


def _prep_tables(seq):
    cos_a, sin_a = _rope_tables(seq, A_HEAD_DIM)
    cos2 = jnp.concatenate([cos_a, cos_a], axis=1)
    sin2 = jnp.concatenate([-sin_a, sin_a], axis=1)
    cos_b, sin_b = _rope_tables(seq, ROPE_DIM)
    half = ROPE_DIM // 2
    z = jnp.zeros((seq, half), F32)
    zz = jnp.zeros((seq, LANES - ROPE_DIM), F32)
    c = jnp.concatenate([cos_b, cos_b, zz], axis=1)
    s1 = jnp.concatenate([-sin_b, z, zz], axis=1)
    s2 = jnp.concatenate([z, sin_b, zz], axis=1)
    return (cos2, sin2), (c, s1, s2)


def _prep_b_weights(w_in, w_q_up, w_kv_up):
    pad = jnp.zeros((w_in.shape[0], LANES - ROPE_DIM), w_in.dtype)
    w_in_p = jnp.concatenate([w_in, pad], axis=1).astype(BF16)
    k = w_q_up.shape[0]
    wq = w_q_up.reshape(k, B_HEADS, NOPE_DIM + ROPE_DIM)
    wq = jnp.concatenate(
        [wq, jnp.zeros((k, B_HEADS, 2 * LANES - NOPE_DIM - ROPE_DIM), wq.dtype)], axis=2)
    wq = wq.reshape(k, B_HEADS * 2 * LANES).astype(BF16)
    wkv = w_kv_up.reshape(k, B_HEADS, NOPE_DIM + V_DIM)
    wk = wkv[:, :, :NOPE_DIM].reshape(k, B_HEADS * NOPE_DIM).astype(BF16)
    wvt = wkv[:, :, NOPE_DIM:].reshape(k, B_HEADS * V_DIM).T.astype(BF16)
    return w_in_p, wq, wk, wvt


def _trunk(x, p, tabs_a, tabs_b):
    depth = p["norm_mix"].shape[0]
    for i in range(depth):
        j = i // 2
        g_mix = p["norm_mix"][i][None, :]
        if i % 2 == 0:
            qkv = _qkv_a(x, g_mix, p["a_w_qkv"][j], *tabs_a)
            o = _win_attn(qkv, p["a_sink2"][j])
            x = _out_proj(o, p["a_w_o"][j], x)
        else:
            w_in_p, wq, wk, wvt = p["b_w"][j]
            cq, ckv, kr = _in_b(x, g_mix, w_in_p, p["b_q_norm"][j][None, :],
                                p["b_kv_norm"][j][None, :], *tabs_b)
            q = _proj_q(cq, wq, tabs_b)
            kn, vt = _proj_kv(ckv, wk, wvt)
            o = _mla_attn(q, kn, kr, vt)
            x = _out_proj(o, p["b_w_o"][j], x)
        x = _ffn(x, p["norm_ffn"][i][None, :], p["f_w_in"][i], p["f_conv_w"][i],
                 p["f_conv_b"][i][None, :], p["f_w_out"][i])
    return _final_norm(x, p["norm_final"][None, :])


def kernel(x_prompt, x_sample, norm_mix, norm_ffn, norm_final, a_w_qkv, a_w_o, a_sink,
           b_w_in, b_q_norm, b_w_q_up, b_kv_norm, b_w_kv_up, b_w_o,
           f_w_in, f_conv_w, f_conv_b, f_w_out):
    p = {
        "norm_mix": norm_mix, "norm_ffn": norm_ffn, "norm_final": norm_final,
        "a_w_qkv": a_w_qkv.astype(BF16), "a_w_o": a_w_o.astype(BF16),
        "a_sink2": a_sink * LOG2E,
        "b_q_norm": b_q_norm, "b_kv_norm": b_kv_norm, "b_w_o": b_w_o.astype(BF16),
        "b_w": [_prep_b_weights(b_w_in[j], b_w_q_up[j], b_w_kv_up[j])
                for j in range(b_w_in.shape[0])],
        "f_w_in": f_w_in.astype(BF16), "f_conv_w": f_conv_w, "f_conv_b": f_conv_b,
        "f_w_out": f_w_out.astype(BF16),
    }
    outs = []
    for x in (x_prompt, x_sample):
        tabs_a, tabs_b = _prep_tables(x.shape[1])
        outs.append(_trunk(x, p, tabs_a, tabs_b))
    return tuple(outs)
```

```python
import functools
import math

import jax
import jax.numpy as jnp
from jax import lax
from jax.experimental import pallas as pl
from jax.experimental.pallas import tpu as pltpu

F32 = jnp.float32
BF16 = jnp.bfloat16

EPS = 1e-6
ROPE_THETA = 10000.0
LANES = 128
BF16_ROWS = 16
WINDOW = 128
A_HEADS = 16
A_KV_HEADS = 4
A_GROUP = A_HEADS // A_KV_HEADS
A_HEAD_DIM = 128
B_HEADS = 16
Q_LORA = 512
KV_LORA = 512
NOPE_DIM = 128
ROPE_DIM = 64
V_DIM = 128
LOG2E = math.log2(math.e)
NEG_BIG = -1e30
VMEM_LIMIT = 56 * 1024 * 1024


def _params(sem):
    return pltpu.CompilerParams(dimension_semantics=sem, vmem_limit_bytes=VMEM_LIMIT)


def _rms(x, g):
    return x * lax.rsqrt(jnp.mean(x * x, axis=-1, keepdims=True) + EPS) * g


def _rope64(x, c, s1, s2):
    return x * c + pltpu.roll(x, 96, 1) * s1 + pltpu.roll(x, 32, 1) * s2


def _qkv_a_kernel(x_ref, g_ref, w_ref, cos_ref, sin_ref, o_ref, xn_ref, *,
                  tn, n_q_tiles, n_rope_tiles, q_scale):
    j = pl.program_id(2)

    @pl.when(j == 0)
    def _():
        xn_ref[...] = _rms(x_ref[0], g_ref[...]).astype(BF16)

    acc = jnp.dot(xn_ref[...], w_ref[...], preferred_element_type=F32)
    mult = jnp.where(j < n_q_tiles, q_scale, 1.0).astype(F32)
    rope = j < n_rope_tiles
    c = jnp.where(rope, cos_ref[...], 1.0) * mult
    s = jnp.where(rope, sin_ref[...], 0.0) * mult
    for h in range(tn // LANES):
        ch = acc[:, h * LANES:(h + 1) * LANES]
        o_ref[0, :, h * LANES:(h + 1) * LANES] = (
            ch * c + pltpu.roll(ch, LANES // 2, 1) * s).astype(BF16)


def _qkv_a(x, g, w, cos2, sin2, *, t=1024, tn=512):
    b, s, d = x.shape
    n = w.shape[1]
    q_cols = A_HEADS * A_HEAD_DIM
    k_cols = A_KV_HEADS * A_HEAD_DIM
    kern = functools.partial(
        _qkv_a_kernel, tn=tn, n_q_tiles=q_cols // tn,
        n_rope_tiles=(q_cols + k_cols) // tn,
        q_scale=A_HEAD_DIM ** -0.5 * LOG2E)
    return pl.pallas_call(
        kern,
        grid=(b, s // t, n // tn),
        in_specs=[
            pl.BlockSpec((1, t, d), lambda bi, i, j: (bi, i, 0)),
            pl.BlockSpec((1, d), lambda bi, i, j: (0, 0)),
            pl.BlockSpec((d, tn), lambda bi, i, j: (0, j)),
            pl.BlockSpec((t, LANES), lambda bi, i, j: (i, 0)),
            pl.BlockSpec((t, LANES), lambda bi, i, j: (i, 0)),
        ],
        out_specs=pl.BlockSpec((1, t, tn), lambda bi, i, j: (bi, i, j)),
        out_shape=jax.ShapeDtypeStruct((b, s, n), BF16),
        scratch_shapes=[pltpu.VMEM((t, d), BF16)],
        compiler_params=_params(("parallel", "parallel", "arbitrary")),
        name="qkv_a",
    )(x, g, w, cos2, sin2)


def _win_attn_kernel(sink_ref, q_ref, km_ref, kp_ref, kx_ref, vm_ref, vp_ref, vx_ref,
                     o_ref, kbuf, vbuf, *, tq, n_tiles):
    i = pl.program_id(1)
    blk = WINDOW
    span = 3 * blk
    kbuf[0:blk, :] = kp_ref[0]
    kbuf[blk:blk + tq, :] = km_ref[0]
    kbuf[blk + tq:, :] = kx_ref[0]
    vbuf[0:blk, :] = vp_ref[0]
    vbuf[blk:blk + tq, :] = vm_ref[0]
    vbuf[blk + tq:, :] = vx_ref[0]

    rows = A_GROUP * blk
    a_idx = lax.broadcasted_iota(jnp.int32, (rows, span), 0) % blk
    c_idx = lax.broadcasted_iota(jnp.int32, (rows, span), 1)
    rel = c_idx - a_idx
    band = (rel >= 0) & (rel <= 2 * WINDOW)
    grp = lax.broadcasted_iota(jnp.int32, (rows, 1), 0) // blk
    lo = jnp.where(i == 0, blk, 0)
    hi = jnp.where(i == n_tiles - 1, 2 * blk, span)

    n_sub = tq // blk
    for r in range(n_sub):
        valid = band
        if r == 0:
            valid = valid & (c_idx >= lo)
        if r == n_sub - 1:
            valid = valid & (c_idx < hi)
        for kh in range(A_KV_HEADS):
            qg = jnp.concatenate(
                [q_ref[0, r * blk:(r + 1) * blk,
                       (kh * A_GROUP + g) * A_HEAD_DIM:(kh * A_GROUP + g + 1) * A_HEAD_DIM]
                 for g in range(A_GROUP)], axis=0)
            kw = kbuf[r * blk:r * blk + span, kh * A_HEAD_DIM:(kh + 1) * A_HEAD_DIM]
            vw = vbuf[r * blk:r * blk + span, kh * A_HEAD_DIM:(kh + 1) * A_HEAD_DIM]
            s = lax.dot_general(qg, kw, (((1,), (1,)), ((), ())),
                                preferred_element_type=F32)
            s = jnp.where(valid, s, NEG_BIG)
            sk = jnp.zeros((rows, 1), F32)
            for g in range(A_GROUP):
                sk = jnp.where(grp == g, sink_ref[kh * A_GROUP + g], sk)
            m = jnp.maximum(jnp.max(s, axis=1, keepdims=True), sk)
            p = jnp.exp2(s - m)
            den = jnp.sum(p, axis=1, keepdims=True) + jnp.exp2(sk - m)
            o = jnp.dot(p.astype(BF16), vw, preferred_element_type=F32) / den
            for g in range(A_GROUP):
                col = (kh * A_GROUP + g) * A_HEAD_DIM
                o_ref[0, r * blk:(r + 1) * blk, col:col + A_HEAD_DIM] = (
                    o[g * blk:(g + 1) * blk].astype(BF16))


def _win_attn(qkv, sink2, *, tq=512):
    b, s, _ = qkv.shape
    d = A_HEADS * A_HEAD_DIM
    kd = A_KV_HEADS * A_HEAD_DIM
    blk = WINDOW
    n_tiles = s // tq
    sub = tq // blk
    nb = s // blk
    k_col = d // kd
    v_col = k_col + 1

    def main(col):
        return pl.BlockSpec((1, tq, kd), lambda bi, i: (bi, i, col))

    def prev(col):
        return pl.BlockSpec((1, blk, kd), lambda bi, i: (bi, jnp.maximum(i * sub - 1, 0), col))

    def nxt(col):
        return pl.BlockSpec((1, blk, kd), lambda bi, i: (bi, jnp.minimum((i + 1) * sub, nb - 1), col))

    kern = functools.partial(_win_attn_kernel, tq=tq, n_tiles=n_tiles)
    return pl.pallas_call(
        kern,
        grid=(b, n_tiles),
        in_specs=[
            pl.BlockSpec(memory_space=pltpu.SMEM),
            pl.BlockSpec((1, tq, d), lambda bi, i: (bi, i, 0)),
            main(k_col), prev(k_col), nxt(k_col),
            main(v_col), prev(v_col), nxt(v_col),
        ],
        out_specs=pl.BlockSpec((1, tq, d), lambda bi, i: (bi, i, 0)),
        out_shape=jax.ShapeDtypeStruct((b, s, d), BF16),
        scratch_shapes=[pltpu.VMEM((tq + 2 * blk, kd), BF16),
                        pltpu.VMEM((tq + 2 * blk, kd), BF16)],
        compiler_params=_params(("parallel", "parallel")),
        name="win_attn",
    )(sink2, qkv, qkv, qkv, qkv, qkv, qkv, qkv)


def _in_b_kernel(x_ref, g_ref, w_ref, qn_ref, kvn_ref, c_ref, s1_ref, s2_ref,
                 cq_ref, ckv_ref, kr_ref):
    xn = _rms(x_ref[0], g_ref[...]).astype(BF16)
    h = jnp.dot(xn, w_ref[...], preferred_element_type=F32)
    cq_ref[0] = _rms(h[:, :Q_LORA], qn_ref[...]).astype(BF16)
    ckv_ref[0] = _rms(h[:, Q_LORA:Q_LORA + KV_LORA], kvn_ref[...]).astype(BF16)
    kr_ref[0] = _rope64(h[:, Q_LORA + KV_LORA:], c_ref[...], s1_ref[...],
                        s2_ref[...]).astype(BF16)


def _in_b(x, g, w, qn, kvn, c, s1, s2, *, t=512):
    b, s, d = x.shape
    n = w.shape[1]
    row = lambda bi, i: (bi, i, 0)
    fixed = lambda bi, i: (0, 0)
    tab = pl.BlockSpec((t, LANES), lambda bi, i: (i, 0))
    return pl.pallas_call(
        _in_b_kernel,
        grid=(b, s // t),
        in_specs=[
            pl.BlockSpec((1, t, d), row),
            pl.BlockSpec((1, d), fixed),
            pl.BlockSpec((d, n), fixed),
            pl.BlockSpec((1, Q_LORA), fixed),
            pl.BlockSpec((1, KV_LORA), fixed),
            tab, tab, tab,
        ],
        out_specs=[pl.BlockSpec((1, t, Q_LORA), row),
                   pl.BlockSpec((1, t, KV_LORA), row),
                   pl.BlockSpec((1, t, LANES), row)],
        out_shape=[jax.ShapeDtypeStruct((b, s, Q_LORA), BF16),
                   jax.ShapeDtypeStruct((b, s, KV_LORA), BF16),
                   jax.ShapeDtypeStruct((b, s, LANES), BF16)],
        compiler_params=_params(("parallel", "parallel")),
        name="in_b",
    )(x, g, w, qn, kvn, c, s1, s2)


def _proj_q_kernel(x_ref, w_ref, c_ref, s1_ref, s2_ref, o_ref, *, tn, scale):
    acc = jnp.dot(x_ref[0], w_ref[...], preferred_element_type=F32)
    c = c_ref[...] * scale
    s1 = s1_ref[...] * scale
    s2 = s2_ref[...] * scale
    hw = 2 * LANES
    for h in range(tn // hw):
        o_ref[0, :, h * hw:h * hw + LANES] = (
            acc[:, h * hw:h * hw + LANES] * scale).astype(BF16)
        o_ref[0, :, h * hw + LANES:(h + 1) * hw] = _rope64(
            acc[:, h * hw + LANES:(h + 1) * hw], c, s1, s2).astype(BF16)


def _proj_q(x, w, tables, *, t=1024, tn=1024):
    b, s, k = x.shape
    n = w.shape[1]
    kern = functools.partial(_proj_q_kernel, tn=tn,
                             scale=(NOPE_DIM + ROPE_DIM) ** -0.5 * LOG2E)
    tab = pl.BlockSpec((t, LANES), lambda bi, i, j: (i, 0))
    return pl.pallas_call(
        kern,
        grid=(b, s // t, n // tn),
        in_specs=[pl.BlockSpec((1, t, k), lambda bi, i, j: (bi, i, 0)),
                  pl.BlockSpec((k, tn), lambda bi, i, j: (0, j)),
                  tab, tab, tab],
        out_specs=pl.BlockSpec((1, t, tn), lambda bi, i, j: (bi, i, j)),
        out_shape=jax.ShapeDtypeStruct((b, s, n), BF16),
        compiler_params=_params(("parallel", "parallel", "parallel")),
        name="proj_q",
    )(x, w, *tables)


def _proj_kv_kernel(x_ref, wk_ref, wvt_ref, kn_ref, vt_ref):
    x = x_ref[0]
    kn_ref[0] = jnp.dot(x, wk_ref[...], preferred_element_type=F32).astype(BF16)
    vt_ref[0] = lax.dot_general(wvt_ref[...], x, (((1,), (1,)), ((), ())),
                                preferred_element_type=F32).astype(BF16)


def _proj_kv(x, wk, wvt, *, t=1024, tn=512):
    b, s, k = x.shape
    n = wk.shape[1]
    return pl.pallas_call(
        _proj_kv_kernel,
        grid=(b, s // t, n // tn),
        in_specs=[pl.BlockSpec((1, t, k), lambda bi, i, j: (bi, i, 0)),
                  pl.BlockSpec((k, tn), lambda bi, i, j: (0, j)),
                  pl.BlockSpec((tn, k), lambda bi, i, j: (j, 0))],
        out_specs=[pl.BlockSpec((1, t, tn), lambda bi, i, j: (bi, i, j)),
                   pl.BlockSpec((1, tn, t), lambda bi, i, j: (bi, j, i))],
        out_shape=[jax.ShapeDtypeStruct((b, s, n), BF16),
                   jax.ShapeDtypeStruct((b, n, s), BF16)],
        compiler_params=_params(("parallel", "parallel", "parallel")),
        name="proj_kv",
    )(x, wk, wvt)


def _mla_kernel(q_ref, kn_ref, kr_ref, vt_ref, o_ref, kcat_ref, st_ref, m_ref, *,
                n_i, kc, n_steps):
    t = pl.program_id(0)
    s_len, tq = st_ref.shape
    sub8 = 8

    @pl.when(t == 0)
    def _():
        st_ref[...] = jnp.zeros(st_ref.shape, F32)
        m_ref[...] = jnp.zeros(m_ref.shape, F32)

    @pl.when((t % n_i == 0) & (t < n_steps))
    def _():
        kcat_ref[:, :LANES] = kn_ref[0]
        kcat_ref[:, LANES:] = kr_ref[0]

    q = q_ref[0]
    m_old = m_ref[...]
    m_new = None
    den8 = None
    ot = None
    for c in range(s_len // kc):
        rows = slice(c * kc, (c + 1) * kc)
        p = jnp.exp2(st_ref[rows, :] - m_old)
        part = jnp.sum(p.reshape(kc // sub8, sub8, tq), axis=0)
        den8 = part if den8 is None else den8 + part
        pv = jnp.dot(vt_ref[0, :, rows], p.astype(BF16), preferred_element_type=F32)
        ot = pv if ot is None else ot + pv
        s_new = lax.dot_general(kcat_ref[rows, :], q, (((1,), (1,)), ((), ())),
                                preferred_element_type=F32)
        st_ref[rows, :] = s_new
        part = jnp.max(s_new.reshape(kc // sub8, sub8, tq), axis=0)
        m_new = part if m_new is None else jnp.maximum(m_new, part)
    m_ref[...] = jnp.max(m_new, axis=0, keepdims=True)
    den = jnp.sum(den8, axis=0, keepdims=True)
    o_ref[0] = (ot / den).T.astype(BF16)


def _mla_attn(q, kn, kr, vt, *, tq=512, kc=512):
    b, s, _ = q.shape
    qw = 2 * LANES
    n_i = s // tq
    n_steps = b * B_HEADS * n_i

    def unflat(t):
        bh = t // n_i
        return bh // B_HEADS, bh % B_HEADS, t % n_i

    def cur(t):
        return unflat(jnp.minimum(t, n_steps - 1))

    def prev(t):
        return unflat(jnp.maximum(t - 1, 0))

    def q_map(t):
        bi, h, i = cur(t)
        return bi, i, h

    def kn_map(t):
        bi, h, _ = cur(t)
        return bi, 0, h

    def kr_map(t):
        bi, _, _ = cur(t)
        return bi, 0, 0

    def vt_map(t):
        bi, h, _ = prev(t)
        return bi, h, 0

    def o_map(t):
        bi, h, i = prev(t)
        return bi, i, h

    kern = functools.partial(_mla_kernel, n_i=n_i, kc=kc, n_steps=n_steps)
    return pl.pallas_call(
        kern,
        grid=(n_steps + 1,),
        in_specs=[
            pl.BlockSpec((1, tq, qw), q_map),
            pl.BlockSpec((1, s, NOPE_DIM), kn_map),
            pl.BlockSpec((1, s, LANES), kr_map),
            pl.BlockSpec((1, V_DIM, s), vt_map),
        ],
        out_specs=pl.BlockSpec((1, tq, V_DIM), o_map),
        out_shape=jax.ShapeDtypeStruct((b, s, B_HEADS * V_DIM), BF16),
        scratch_shapes=[pltpu.VMEM((s, qw), BF16),
                        pltpu.VMEM((s, tq), F32),
                        pltpu.VMEM((1, tq), F32)],
        compiler_params=_params(("arbitrary",)),
        name="mla_attn",
    )(q, kn, kr, vt)


def _out_proj_kernel(o_ref, w_ref, x_ref, y_ref):
    y_ref[0] = x_ref[0] + jnp.dot(o_ref[0], w_ref[...], preferred_element_type=F32)


def _out_proj(o, w, x, *, t=1024, tn=512):
    b, s, k = o.shape
    n = w.shape[1]
    return pl.pallas_call(
        _out_proj_kernel,
        grid=(b, s // t, n // tn),
        in_specs=[pl.BlockSpec((1, t, k), lambda bi, i, j: (bi, i, 0)),
                  pl.BlockSpec((k, tn), lambda bi, i, j: (0, j)),
                  pl.BlockSpec((1, t, tn), lambda bi, i, j: (bi, i, j))],
        out_specs=pl.BlockSpec((1, t, tn), lambda bi, i, j: (bi, i, j)),
        out_shape=jax.ShapeDtypeStruct((b, s, n), F32),
        compiler_params=_params(("parallel", "parallel", "parallel")),
        name="out_proj",
    )(o, w, x)


def _ffn_kernel(xp_ref, x_ref, xx_ref, g_ref, wg_ref, wu_ref, cwg_ref, cwu_ref,
                cbg_ref, cbu_ref, wo_ref, y_ref, xn_ref, a0_ref, a1_ref, hg0_ref, hg1_ref,
                hu0_ref, hu1_ref, *, t, n_tiles, nf, n_steps):
    k = pl.program_id(0)
    f = k % nf
    i = (k // nf) % n_tiles
    halo = BF16_ROWS

    a_refs = (a0_ref, a1_ref)
    hg_refs = (hg0_ref, hg1_ref)
    hu_refs = (hu0_ref, hu1_ref)

    @pl.when(k == 0)
    def _():
        for ref in a_refs + hg_refs + hu_refs + (y_ref,):
            ref[...] = jnp.zeros(ref.shape, ref.dtype)

    @pl.when((f == 0) & (k < n_steps))
    def _():
        g = g_ref[...]
        prev = jnp.where(i > 0, _rms(xp_ref[0], g), 0.0)
        nxt = jnp.where(i < n_tiles - 1, _rms(xx_ref[0], g), 0.0)
        xn_ref[0:halo, :] = prev.astype(BF16)
        xn_ref[halo:halo + t, :] = _rms(x_ref[0], g).astype(BF16)
        xn_ref[halo + t:, :] = nxt.astype(BF16)

    @pl.when(f == 2)
    def _():
        y_ref[0] = x_ref[0]

    tf = a0_ref.shape[1]
    d = y_ref.shape[2]
    rb = 64
    blocks = [(r, c) for c in range(tf // LANES) for r in range(t // rb)]
    m_rows = t + 2 * halo
    up_rows = m_rows // (3 if m_rows % (3 * BF16_ROWS) == 0 else 2)
    down_rows = min(t, 512)
    down_cols = 512

    def conv(h_ref, r, cols, cw_ref, cb_ref):
        cw = cw_ref[:, cols]
        lo = halo + r * rb
        return (h_ref[lo - 1:lo - 1 + rb, cols] * cw[0:1]
                + h_ref[lo:lo + rb, cols] * cw[1:2]
                + h_ref[lo + 1:lo + 1 + rb, cols] * cw[2:3] + cb_ref[:, cols])

    def stages(p):
        def gate_block(r, c):
            cols = slice(c * LANES, (c + 1) * LANES)
            gc = conv(hg_refs[1 - p], r, cols, cwg_ref, cbg_ref)
            uc = conv(hu_refs[1 - p], r, cols, cwu_ref, cbu_ref)
            a_refs[1 - p][r * rb:(r + 1) * rb, cols] = (
                gc * jax.nn.sigmoid(gc) * uc).astype(BF16)

        def up_piece(h_ref, w_ref, lo):
            h_ref[lo:lo + up_rows, :] = jnp.dot(xn_ref[lo:lo + up_rows, :], w_ref[0],
                                                preferred_element_type=F32)

        def down_piece(r0, c0):
            rows = slice(r0, r0 + down_rows)
            cols = slice(c0, c0 + down_cols)
            y_ref[0, rows, cols] += jnp.dot(a_refs[p][rows, :], wo_ref[:, cols],
                                            preferred_element_type=F32)

        pieces = [(functools.partial(up_piece, h_ref, w_ref, lo), up_rows * d)
                  for lo in range(0, m_rows, up_rows)
                  for h_ref, w_ref in ((hg_refs[p], wg_ref), (hu_refs[p], wu_ref))]
        pieces += [(functools.partial(down_piece, r0, c0), down_rows * down_cols)
                   for r0 in range(0, t, down_rows) for c0 in range(0, d, down_cols)]
        total = sum(w for _, w in pieces)
        done = 0
        acc = 0
        for piece, w in pieces:
            piece()
            acc += w
            upto = len(blocks) * acc // total
            for r, c in blocks[done:upto]:
                gate_block(r, c)
            done = upto

    @pl.when(k % 2 == 0)
    def _():
        stages(0)

    @pl.when(k % 2 == 1)
    def _():
        stages(1)


FFN_TF = 256


def _ffn(x, g, w_in_tiles, conv_w, conv_b, w_out, *, t=1024):
    b, s, d = x.shape
    d_ff = w_out.shape[0]
    tf = w_in_tiles.shape[2]
    nf = d_ff // tf
    assert nf >= 3 and w_in_tiles.shape[0] == 2 * nf
    n_tiles = s // t
    n_steps = b * n_tiles * nf
    halo = BF16_ROWS
    hb = t // halo
    last_hb = s // halo - 1

    def item(st, lag):
        st = jnp.clip(st - lag, 0, n_steps - 1)
        tile = st // nf
        return tile // n_tiles, tile % n_tiles, st % nf

    def x_map(st):
        bi, i, _ = item(st, 0)
        return bi, i, 0

    def xp_map(st):
        bi, i, _ = item(st, 0)
        return bi, jnp.maximum(i * hb - 1, 0), 0

    def xx_map(st):
        bi, i, _ = item(st, 0)
        return bi, jnp.minimum((i + 1) * hb, last_hb), 0

    def gate(lag):
        return lambda st: (0, item(st, lag)[2])

    def up(lag):
        return lambda st: (0, nf + item(st, lag)[2])

    def wo_map(st):
        return item(st, 2)[2], 0

    def y_map(st):
        bi, i, _ = item(st, 2)
        return bi, i, 0

    kern = functools.partial(_ffn_kernel, t=t, n_tiles=n_tiles, nf=nf, n_steps=n_steps)
    return pl.pallas_call(
        kern,
        grid=(n_steps + 2,),
        in_specs=[
            pl.BlockSpec((1, halo, d), xp_map),
            pl.BlockSpec((1, t, d), x_map),
            pl.BlockSpec((1, halo, d), xx_map),
            pl.BlockSpec((1, d), lambda st: (0, 0)),
            pl.BlockSpec((1, d, tf), lambda st: (item(st, 0)[2], 0, 0)),
            pl.BlockSpec((1, d, tf), lambda st: (nf + item(st, 0)[2], 0, 0)),
            pl.BlockSpec((3, tf), gate(1)),
            pl.BlockSpec((3, tf), up(1)),
            pl.BlockSpec((1, tf), gate(1)),
            pl.BlockSpec((1, tf), up(1)),
            pl.BlockSpec((tf, d), wo_map),
        ],
        out_specs=pl.BlockSpec((1, t, d), y_map),
        out_shape=jax.ShapeDtypeStruct((b, s, d), F32),
        scratch_shapes=[pltpu.VMEM((t + 2 * halo, d), BF16)]
        + [pltpu.VMEM((t, tf), BF16)] * 2
        + [pltpu.VMEM((t + 2 * halo, tf), F32)] * 4,
        compiler_params=_params(("arbitrary",)),
        name="ffn",
    )(x, x, x, g, w_in_tiles, w_in_tiles, conv_w, conv_w, conv_b, conv_b, w_out)


def _final_norm_kernel(x_ref, g_ref, y_ref):
    y_ref[0] = _rms(x_ref[0], g_ref[...])


def _final_norm(x, g, *, t=1024):
    b, s, d = x.shape
    return pl.pallas_call(
        _final_norm_kernel,
        grid=(b, s // t),
        in_specs=[pl.BlockSpec((1, t, d), lambda bi, i: (bi, i, 0)),
                  pl.BlockSpec((1, d), lambda bi, i: (0, 0))],
        out_specs=pl.BlockSpec((1, t, d), lambda bi, i: (bi, i, 0)),
        out_shape=jax.ShapeDtypeStruct((b, s, d), F32),
        compiler_params=_params(("parallel", "parallel")),
        name="final_norm",
    )(x, g)


def _rope_tables(seq, dim):
    pos = jnp.arange(seq, dtype=F32)
    inv = 1.0 / (ROPE_THETA ** (jnp.arange(0, dim, 2, dtype=F32) / dim))
    ang = pos[:, None] * inv[None, :]
    return jnp.cos(ang), jnp.sin(ang)


def _prep_tables(seq):
    cos_a, sin_a = _rope_tables(seq, A_HEAD_DIM)
    cos2 = jnp.concatenate([cos_a, cos_a], axis=1)
    sin2 = jnp.concatenate([-sin_a, sin_a], axis=1)
    cos_b, sin_b = _rope_tables(seq, ROPE_DIM)
    half = ROPE_DIM // 2
    z = jnp.zeros((seq, half), F32)
    zz = jnp.zeros((seq, LANES - ROPE_DIM), F32)
    c = jnp.concatenate([cos_b, cos_b, zz], axis=1)
    s1 = jnp.concatenate([-sin_b, z, zz], axis=1)
    s2 = jnp.concatenate([z, sin_b, zz], axis=1)
    return (cos2, sin2), (c, s1, s2)


def _tile_cols(w, tn):
    layers, k, n = w.shape
    return w.astype(BF16).reshape(layers, k, n // tn, tn).transpose(0, 2, 1, 3)


def _prep_b_weights(w_in, w_q_up, w_kv_up):
    pad = jnp.zeros((w_in.shape[0], LANES - ROPE_DIM), w_in.dtype)
    w_in_p = jnp.concatenate([w_in, pad], axis=1).astype(BF16)
    k = w_q_up.shape[0]
    wq = w_q_up.reshape(k, B_HEADS, NOPE_DIM + ROPE_DIM)
    wq = jnp.concatenate(
        [wq, jnp.zeros((k, B_HEADS, 2 * LANES - NOPE_DIM - ROPE_DIM), wq.dtype)], axis=2)
    wq = wq.reshape(k, B_HEADS * 2 * LANES).astype(BF16)
    wkv = w_kv_up.reshape(k, B_HEADS, NOPE_DIM + V_DIM)
    wk = wkv[:, :, :NOPE_DIM].reshape(k, B_HEADS * NOPE_DIM).astype(BF16)
    wvt = wkv[:, :, NOPE_DIM:].reshape(k, B_HEADS * V_DIM).T.astype(BF16)
    return w_in_p, wq, wk, wvt


def _trunk(x, p, tabs_a, tabs_b):
    depth = p["norm_mix"].shape[0]
    for i in range(depth):
        j = i // 2
        g_mix = p["norm_mix"][i][None, :]
        if i % 2 == 0:
            qkv = _qkv_a(x, g_mix, p["a_w_qkv"][j], *tabs_a)
            o = _win_attn(qkv, p["a_sink2"][j])
            x = _out_proj(o, p["a_w_o"][j], x)
        else:
            w_in_p, wq, wk, wvt = p["b_w"][j]
            cq, ckv, kr = _in_b(x, g_mix, w_in_p, p["b_q_norm"][j][None, :],
                                p["b_kv_norm"][j][None, :], *tabs_b)
            q = _proj_q(cq, wq, tabs_b)
            kn, vt = _proj_kv(ckv, wk, wvt)
            o = _mla_attn(q, kn, kr, vt)
            x = _out_proj(o, p["b_w_o"][j], x)
        x = _ffn(x, p["norm_ffn"][i][None, :], p["f_w_in"][i], p["f_conv_w"][i],
                 p["f_conv_b"][i][None, :], p["f_w_out"][i])
    return _final_norm(x, p["norm_final"][None, :])


def kernel(x_prompt, x_sample, norm_mix, norm_ffn, norm_final, a_w_qkv, a_w_o, a_sink,
           b_w_in, b_q_norm, b_w_q_up, b_kv_norm, b_w_kv_up, b_w_o,
           f_w_in, f_conv_w, f_conv_b, f_w_out):
    p = {
        "norm_mix": norm_mix, "norm_ffn": norm_ffn, "norm_final": norm_final,
        "a_w_qkv": a_w_qkv.astype(BF16), "a_w_o": a_w_o.astype(BF16),
        "a_sink2": a_sink * LOG2E,
        "b_q_norm": b_q_norm, "b_kv_norm": b_kv_norm, "b_w_o": b_w_o.astype(BF16),
        "b_w": [_prep_b_weights(b_w_in[j], b_w_q_up[j], b_w_kv_up[j])
                for j in range(b_w_in.shape[0])],
        "f_w_in": _tile_cols(f_w_in, FFN_TF), "f_conv_w": f_conv_w, "f_conv_b": f_conv_b,
        "f_w_out": f_w_out.astype(BF16),
    }
    outs = []
    for x in (x_prompt, x_sample):
        tabs_a, tabs_b = _prep_tables(x.shape[1])
        outs.append(_trunk(x, p, tabs_a, tabs_b))
    return tuple(outs)
```

```python
import functools
import math

import jax
import jax.numpy as jnp
from jax import lax
from jax.experimental import pallas as pl
from jax.experimental.pallas import tpu as pltpu

F32 = jnp.float32
BF16 = jnp.bfloat16

EPS = 1e-6
ROPE_THETA = 10000.0
LANES = 128
BF16_ROWS = 16
WINDOW = 128
A_HEADS = 16
A_KV_HEADS = 4
A_GROUP = A_HEADS // A_KV_HEADS
A_HEAD_DIM = 128
B_HEADS = 16
Q_LORA = 512
KV_LORA = 512
NOPE_DIM = 128
ROPE_DIM = 64
V_DIM = 128
LOG2E = math.log2(math.e)
NEG_BIG = -1e30
VMEM_LIMIT = 56 * 1024 * 1024


def _params(sem):
    return pltpu.CompilerParams(dimension_semantics=sem, vmem_limit_bytes=VMEM_LIMIT)


def _rms(x, g):
    return x * lax.rsqrt(jnp.mean(x * x, axis=-1, keepdims=True) + EPS) * g


def _rope64(x, c, s1, s2):
    return x * c + pltpu.roll(x, 96, 1) * s1 + pltpu.roll(x, 32, 1) * s2


def _qkv_a_kernel(x_ref, g_ref, w_ref, cos_ref, sin_ref, o_ref, xn_ref, *,
                  tn, q_cols, rope_cols, q_scale):
    xn_ref[...] = _rms(x_ref[0], g_ref[...]).astype(BF16)
    cos = cos_ref[...]
    sin = sin_ref[...]
    for j in range(w_ref.shape[1] // tn):
        acc = jnp.dot(xn_ref[...], w_ref[:, j * tn:(j + 1) * tn],
                      preferred_element_type=F32)
        for h in range(tn // LANES):
            col = j * tn + h * LANES
            ch = acc[:, h * LANES:(h + 1) * LANES]
            if col < rope_cols:
                ch = ch * cos + pltpu.roll(ch, LANES // 2, 1) * sin
            if col < q_cols:
                ch = ch * q_scale
            o_ref[0, :, col:col + LANES] = ch.astype(BF16)


def _qkv_a(x, g, w, cos2, sin2, *, t=512, tn=512):
    b, s, d = x.shape
    n = w.shape[1]
    q_cols = A_HEADS * A_HEAD_DIM
    k_cols = A_KV_HEADS * A_HEAD_DIM
    kern = functools.partial(
        _qkv_a_kernel, tn=tn, q_cols=q_cols, rope_cols=q_cols + k_cols,
        q_scale=A_HEAD_DIM ** -0.5 * LOG2E)
    return pl.pallas_call(
        kern,
        grid=(b, s // t),
        in_specs=[
            pl.BlockSpec((1, t, d), lambda bi, i: (bi, i, 0)),
            pl.BlockSpec((1, d), lambda bi, i: (0, 0)),
            pl.BlockSpec((d, n), lambda bi, i: (0, 0)),
            pl.BlockSpec((t, LANES), lambda bi, i: (i, 0)),
            pl.BlockSpec((t, LANES), lambda bi, i: (i, 0)),
        ],
        out_specs=pl.BlockSpec((1, t, n), lambda bi, i: (bi, i, 0)),
        out_shape=jax.ShapeDtypeStruct((b, s, n), BF16),
        scratch_shapes=[pltpu.VMEM((t, d), BF16)],
        compiler_params=_params(("parallel", "parallel")),
        name="qkv_a",
    )(x, g, w, cos2, sin2)


def _win_attn_kernel(sink_ref, q_ref, km_ref, kp_ref, kx_ref, vm_ref, vp_ref, vx_ref,
                     o_ref, kbuf, vbuf, *, tq, n_tiles):
    i = pl.program_id(1)
    blk = WINDOW
    span = 3 * blk
    kbuf[0:blk, :] = kp_ref[0]
    kbuf[blk:blk + tq, :] = km_ref[0]
    kbuf[blk + tq:, :] = kx_ref[0]
    vbuf[0:blk, :] = vp_ref[0]
    vbuf[blk:blk + tq, :] = vm_ref[0]
    vbuf[blk + tq:, :] = vx_ref[0]

    rows = A_GROUP * blk
    a_idx = lax.broadcasted_iota(jnp.int32, (rows, span), 0) % blk
    c_idx = lax.broadcasted_iota(jnp.int32, (rows, span), 1)
    rel = c_idx - a_idx
    band = (rel >= 0) & (rel <= 2 * WINDOW)
    grp = lax.broadcasted_iota(jnp.int32, (rows, 1), 0) // blk
    lo = jnp.where(i == 0, blk, 0)
    hi = jnp.where(i == n_tiles - 1, 2 * blk, span)

    n_sub = tq // blk
    for r in range(n_sub):
        valid = band
        if r == 0:
            valid = valid & (c_idx >= lo)
        if r == n_sub - 1:
            valid = valid & (c_idx < hi)
        for kh in range(A_KV_HEADS):
            qg = jnp.concatenate(
                [q_ref[0, r * blk:(r + 1) * blk,
                       (kh * A_GROUP + g) * A_HEAD_DIM:(kh * A_GROUP + g + 1) * A_HEAD_DIM]
                 for g in range(A_GROUP)], axis=0)
            kw = kbuf[r * blk:r * blk + span, kh * A_HEAD_DIM:(kh + 1) * A_HEAD_DIM]
            vw = vbuf[r * blk:r * blk + span, kh * A_HEAD_DIM:(kh + 1) * A_HEAD_DIM]
            s = lax.dot_general(qg, kw, (((1,), (1,)), ((), ())),
                                preferred_element_type=F32)
            s = jnp.where(valid, s, NEG_BIG)
            sk = jnp.zeros((rows, 1), F32)
            for g in range(A_GROUP):
                sk = jnp.where(grp == g, sink_ref[kh * A_GROUP + g], sk)
            m = jnp.maximum(jnp.max(s, axis=1, keepdims=True), sk)
            p = jnp.exp2(s - m)
            den = jnp.sum(p, axis=1, keepdims=True) + jnp.exp2(sk - m)
            o = jnp.dot(p.astype(BF16), vw, preferred_element_type=F32) / den
            for g in range(A_GROUP):
                col = (kh * A_GROUP + g) * A_HEAD_DIM
                o_ref[0, r * blk:(r + 1) * blk, col:col + A_HEAD_DIM] = (
                    o[g * blk:(g + 1) * blk].astype(BF16))


def _win_attn(qkv, sink2, *, tq=512):
    b, s, _ = qkv.shape
    d = A_HEADS * A_HEAD_DIM
    kd = A_KV_HEADS * A_HEAD_DIM
    blk = WINDOW
    n_tiles = s // tq
    sub = tq // blk
    nb = s // blk
    k_col = d // kd
    v_col = k_col + 1

    def main(col):
        return pl.BlockSpec((1, tq, kd), lambda bi, i: (bi, i, col))

    def prev(col):
        return pl.BlockSpec((1, blk, kd), lambda bi, i: (bi, jnp.maximum(i * sub - 1, 0), col))

    def nxt(col):
        return pl.BlockSpec((1, blk, kd), lambda bi, i: (bi, jnp.minimum((i + 1) * sub, nb - 1), col))

    kern = functools.partial(_win_attn_kernel, tq=tq, n_tiles=n_tiles)
    return pl.pallas_call(
        kern,
        grid=(b, n_tiles),
        in_specs=[
            pl.BlockSpec(memory_space=pltpu.SMEM),
            pl.BlockSpec((1, tq, d), lambda bi, i: (bi, i, 0)),
            main(k_col), prev(k_col), nxt(k_col),
            main(v_col), prev(v_col), nxt(v_col),
        ],
        out_specs=pl.BlockSpec((1, tq, d), lambda bi, i: (bi, i, 0)),
        out_shape=jax.ShapeDtypeStruct((b, s, d), BF16),
        scratch_shapes=[pltpu.VMEM((tq + 2 * blk, kd), BF16),
                        pltpu.VMEM((tq + 2 * blk, kd), BF16)],
        compiler_params=_params(("parallel", "parallel")),
        name="win_attn",
    )(sink2, qkv, qkv, qkv, qkv, qkv, qkv, qkv)


def _in_b_kernel(x_ref, g_ref, w_ref, qn_ref, kvn_ref, c_ref, s1_ref, s2_ref,
                 cq_ref, ckv_ref, kr_ref):
    xn = _rms(x_ref[0], g_ref[...]).astype(BF16)
    h = jnp.dot(xn, w_ref[...], preferred_element_type=F32)
    cq_ref[0] = _rms(h[:, :Q_LORA], qn_ref[...]).astype(BF16)
    ckv_ref[0] = _rms(h[:, Q_LORA:Q_LORA + KV_LORA], kvn_ref[...]).astype(BF16)
    kr_ref[0] = _rope64(h[:, Q_LORA + KV_LORA:], c_ref[...], s1_ref[...],
                        s2_ref[...]).astype(BF16)


def _in_b(x, g, w, qn, kvn, c, s1, s2, *, t=512):
    b, s, d = x.shape
    n = w.shape[1]
    row = lambda bi, i: (bi, i, 0)
    fixed = lambda bi, i: (0, 0)
    tab = pl.BlockSpec((t, LANES), lambda bi, i: (i, 0))
    return pl.pallas_call(
        _in_b_kernel,
        grid=(b, s // t),
        in_specs=[
            pl.BlockSpec((1, t, d), row),
            pl.BlockSpec((1, d), fixed),
            pl.BlockSpec((d, n), fixed),
            pl.BlockSpec((1, Q_LORA), fixed),
            pl.BlockSpec((1, KV_LORA), fixed),
            tab, tab, tab,
        ],
        out_specs=[pl.BlockSpec((1, t, Q_LORA), row),
                   pl.BlockSpec((1, t, KV_LORA), row),
                   pl.BlockSpec((1, t, LANES), row)],
        out_shape=[jax.ShapeDtypeStruct((b, s, Q_LORA), BF16),
                   jax.ShapeDtypeStruct((b, s, KV_LORA), BF16),
                   jax.ShapeDtypeStruct((b, s, LANES), BF16)],
        compiler_params=_params(("parallel", "parallel")),
        name="in_b",
    )(x, g, w, qn, kvn, c, s1, s2)


def _proj_q_kernel(x_ref, w_ref, c_ref, s1_ref, s2_ref, o_ref, *, tn, scale):
    acc = jnp.dot(x_ref[0], w_ref[...], preferred_element_type=F32)
    c = c_ref[...] * scale
    s1 = s1_ref[...] * scale
    s2 = s2_ref[...] * scale
    hw = 2 * LANES
    for h in range(tn // hw):
        o_ref[0, :, h * hw:h * hw + LANES] = (
            acc[:, h * hw:h * hw + LANES] * scale).astype(BF16)
        o_ref[0, :, h * hw + LANES:(h + 1) * hw] = _rope64(
            acc[:, h * hw + LANES:(h + 1) * hw], c, s1, s2).astype(BF16)


def _proj_q(x, w, tables, *, t=1024, tn=1024):
    b, s, k = x.shape
    n = w.shape[1]
    kern = functools.partial(_proj_q_kernel, tn=tn,
                             scale=(NOPE_DIM + ROPE_DIM) ** -0.5 * LOG2E)
    tab = pl.BlockSpec((t, LANES), lambda bi, i, j: (i, 0))
    return pl.pallas_call(
        kern,
        grid=(b, s // t, n // tn),
        in_specs=[pl.BlockSpec((1, t, k), lambda bi, i, j: (bi, i, 0)),
                  pl.BlockSpec((k, tn), lambda bi, i, j: (0, j)),
                  tab, tab, tab],
        out_specs=pl.BlockSpec((1, t, tn), lambda bi, i, j: (bi, i, j)),
        out_shape=jax.ShapeDtypeStruct((b, s, n), BF16),
        compiler_params=_params(("parallel", "parallel", "parallel")),
        name="proj_q",
    )(x, w, *tables)


def _proj_kv_kernel(x_ref, wk_ref, wvt_ref, kn_ref, vt_ref):
    x = x_ref[0]
    kn_ref[0] = jnp.dot(x, wk_ref[...], preferred_element_type=F32).astype(BF16)
    vt_ref[0] = lax.dot_general(wvt_ref[...], x, (((1,), (1,)), ((), ())),
                                preferred_element_type=F32).astype(BF16)


def _proj_kv(x, wk, wvt, *, t=1024, tn=512):
    b, s, k = x.shape
    n = wk.shape[1]
    return pl.pallas_call(
        _proj_kv_kernel,
        grid=(b, s // t, n // tn),
        in_specs=[pl.BlockSpec((1, t, k), lambda bi, i, j: (bi, i, 0)),
                  pl.BlockSpec((k, tn), lambda bi, i, j: (0, j)),
                  pl.BlockSpec((tn, k), lambda bi, i, j: (j, 0))],
        out_specs=[pl.BlockSpec((1, t, tn), lambda bi, i, j: (bi, i, j)),
                   pl.BlockSpec((1, tn, t), lambda bi, i, j: (bi, j, i))],
        out_shape=[jax.ShapeDtypeStruct((b, s, n), BF16),
                   jax.ShapeDtypeStruct((b, n, s), BF16)],
        compiler_params=_params(("parallel", "parallel", "parallel")),
        name="proj_kv",
    )(x, wk, wvt)


def _mla_kernel(q_ref, kn_ref, kr_ref, vt_ref, o_ref, kcat_ref, st_ref, m_ref, *,
                n_i, kc, n_steps):
    t = pl.program_id(0)
    s_len, tq = st_ref.shape
    sub8 = 8

    @pl.when(t == 0)
    def _():
        st_ref[...] = jnp.zeros(st_ref.shape, F32)
        m_ref[...] = jnp.zeros(m_ref.shape, F32)

    @pl.when((t % n_i == 0) & (t < n_steps))
    def _():
        kcat_ref[:, :LANES] = kn_ref[0]
        kcat_ref[:, LANES:] = kr_ref[0]

    q = q_ref[0]
    m_old = m_ref[...]
    m_new = None
    den8 = None
    ot = None
    for c in range(s_len // kc):
        rows = slice(c * kc, (c + 1) * kc)
        p = jnp.exp2(st_ref[rows, :] - m_old)
        part = jnp.sum(p.reshape(kc // sub8, sub8, tq), axis=0)
        den8 = part if den8 is None else den8 + part
        pv = jnp.dot(vt_ref[0, :, rows], p.astype(BF16), preferred_element_type=F32)
        ot = pv if ot is None else ot + pv
        s_new = lax.dot_general(kcat_ref[rows, :], q, (((1,), (1,)), ((), ())),
                                preferred_element_type=F32)
        st_ref[rows, :] = s_new
        part = jnp.max(s_new.reshape(kc // sub8, sub8, tq), axis=0)
        m_new = part if m_new is None else jnp.maximum(m_new, part)
    m_ref[...] = jnp.max(m_new, axis=0, keepdims=True)
    den = jnp.sum(den8, axis=0, keepdims=True)
    o_ref[0] = (ot / den).T.astype(BF16)


def _mla_attn(q, kn, kr, vt, *, tq=512, kc=512):
    b, s, _ = q.shape
    qw = 2 * LANES
    n_i = s // tq
    n_steps = b * B_HEADS * n_i

    def item(t, lag):
        t = jnp.clip(t - lag, 0, n_steps - 1)
        bh = t // n_i
        return bh // B_HEADS, bh % B_HEADS, t % n_i

    def q_map(t):
        bi, h, i = item(t, 0)
        return bi, i, h

    def kn_map(t):
        bi, h, _ = item(t, 0)
        return bi, 0, h

    def kr_map(t):
        bi, _, _ = item(t, 0)
        return bi, 0, 0

    def vt_map(t):
        bi, h, _ = item(t, 1)
        return bi, h, 0

    def o_map(t):
        bi, h, i = item(t, 1)
        return bi, i, h

    kern = functools.partial(_mla_kernel, n_i=n_i, kc=kc, n_steps=n_steps)
    return pl.pallas_call(
        kern,
        grid=(n_steps + 1,),
        in_specs=[
            pl.BlockSpec((1, tq, qw), q_map),
            pl.BlockSpec((1, s, NOPE_DIM), kn_map),
            pl.BlockSpec((1, s, LANES), kr_map),
            pl.BlockSpec((1, V_DIM, s), vt_map),
        ],
        out_specs=pl.BlockSpec((1, tq, V_DIM), o_map),
        out_shape=jax.ShapeDtypeStruct((b, s, B_HEADS * V_DIM), BF16),
        scratch_shapes=[pltpu.VMEM((s, qw), BF16),
                        pltpu.VMEM((s, tq), F32),
                        pltpu.VMEM((1, tq), F32)],
        compiler_params=_params(("arbitrary",)),
        name="mla_attn",
    )(q, kn, kr, vt)


def _out_proj_kernel(o_ref, w_ref, x_ref, y_ref, *, tn):
    for j in range(w_ref.shape[1] // tn):
        cols = slice(j * tn, (j + 1) * tn)
        y_ref[0, :, cols] = x_ref[0, :, cols] + jnp.dot(
            o_ref[0], w_ref[:, cols], preferred_element_type=F32)


def _out_proj(o, w, x, *, t=512, tn=512):
    b, s, k = o.shape
    n = w.shape[1]
    return pl.pallas_call(
        functools.partial(_out_proj_kernel, tn=tn),
        grid=(b, s // t),
        in_specs=[pl.BlockSpec((1, t, k), lambda bi, i: (bi, i, 0)),
                  pl.BlockSpec((k, n), lambda bi, i: (0, 0)),
                  pl.BlockSpec((1, t, n), lambda bi, i: (bi, i, 0))],
        out_specs=pl.BlockSpec((1, t, n), lambda bi, i: (bi, i, 0)),
        out_shape=jax.ShapeDtypeStruct((b, s, n), F32),
        compiler_params=_params(("parallel", "parallel")),
        name="out_proj",
    )(o, w, x)


def _ffn_kernel(xp_ref, x_ref, xx_ref, g_ref, wg_ref, wu_ref, cwg_ref, cwu_ref,
                cbg_ref, cbu_ref, wo_ref, y_ref, xn_ref, a0_ref, a1_ref, hg0_ref, hg1_ref,
                hu0_ref, hu1_ref, *, t, n_tiles, nf, n_steps):
    k = pl.program_id(0)
    f = k % nf
    i = (k // nf) % n_tiles
    halo = BF16_ROWS

    a_refs = (a0_ref, a1_ref)
    hg_refs = (hg0_ref, hg1_ref)
    hu_refs = (hu0_ref, hu1_ref)

    @pl.when(k == 0)
    def _():
        for ref in a_refs + hg_refs + hu_refs + (y_ref,):
            ref[...] = jnp.zeros(ref.shape, ref.dtype)

    @pl.when((f == 0) & (k < n_steps))
    def _():
        g = g_ref[...]
        prev = jnp.where(i > 0, _rms(xp_ref[0], g), 0.0)
        nxt = jnp.where(i < n_tiles - 1, _rms(xx_ref[0], g), 0.0)
        xn_ref[0:halo, :] = prev.astype(BF16)
        xn_ref[halo:halo + t, :] = _rms(x_ref[0], g).astype(BF16)
        xn_ref[halo + t:, :] = nxt.astype(BF16)

    @pl.when(f == 2)
    def _():
        y_ref[0] = x_ref[0]

    tf = a0_ref.shape[1]
    d = y_ref.shape[2]
    rb = 64
    blocks = [(r, c) for c in range(tf // LANES) for r in range(t // rb)]
    m_rows = t + 2 * halo
    up_rows = m_rows // (3 if m_rows % (3 * BF16_ROWS) == 0 else 2)
    down_rows = min(t, 512)
    down_cols = 512

    def conv(h_ref, r, cols, cw_ref, cb_ref):
        cw = cw_ref[:, cols]
        lo = halo + r * rb
        blk = h_ref[lo - 8:lo + rb + 8, cols]
        return (blk[7:7 + rb] * cw[0:1] + blk[8:8 + rb] * cw[1:2]
                + blk[9:9 + rb] * cw[2:3] + cb_ref[:, cols])

    def stages(p):
        def gate_block(r, c):
            cols = slice(c * LANES, (c + 1) * LANES)
            gc = conv(hg_refs[1 - p], r, cols, cwg_ref, cbg_ref)
            uc = conv(hu_refs[1 - p], r, cols, cwu_ref, cbu_ref)
            a_refs[1 - p][r * rb:(r + 1) * rb, cols] = (
                gc * jax.nn.sigmoid(gc) * uc).astype(BF16)

        def up_piece(h_ref, w_ref, lo):
            h_ref[lo:lo + up_rows, :] = jnp.dot(xn_ref[lo:lo + up_rows, :], w_ref[0],
                                                preferred_element_type=F32)

        def down_piece(r0, c0):
            rows = slice(r0, r0 + down_rows)
            cols = slice(c0, c0 + down_cols)
            y_ref[0, rows, cols] += jnp.dot(a_refs[p][rows, :], wo_ref[:, cols],
                                            preferred_element_type=F32)

        pieces = [(functools.partial(up_piece, h_ref, w_ref, lo), up_rows * d)
                  for lo in range(0, m_rows, up_rows)
                  for h_ref, w_ref in ((hg_refs[p], wg_ref), (hu_refs[p], wu_ref))]
        pieces += [(functools.partial(down_piece, r0, c0), down_rows * down_cols)
                   for r0 in range(0, t, down_rows) for c0 in range(0, d, down_cols)]
        total = sum(w for _, w in pieces)
        done = 0
        acc = 0
        for piece, w in pieces:
            piece()
            acc += w
            upto = len(blocks) * acc // total
            for r, c in blocks[done:upto]:
                gate_block(r, c)
            done = upto

    @pl.when(k % 2 == 0)
    def _():
        stages(0)

    @pl.when(k % 2 == 1)
    def _():
        stages(1)


FFN_TF = 256


def _ffn(x, g, w_in_tiles, conv_w, conv_b, w_out, *, t=1024):
    b, s, d = x.shape
    d_ff = w_out.shape[0]
    tf = w_in_tiles.shape[2]
    nf = d_ff // tf
    assert nf >= 3 and w_in_tiles.shape[0] == 2 * nf
    n_tiles = s // t
    n_steps = b * n_tiles * nf
    halo = BF16_ROWS
    hb = t // halo
    last_hb = s // halo - 1

    def item(st, lag):
        st = jnp.clip(st - lag, 0, n_steps - 1)
        tile = st // nf
        return tile // n_tiles, tile % n_tiles, st % nf

    def x_map(st):
        bi, i, _ = item(st, 0)
        return bi, i, 0

    def xp_map(st):
        bi, i, _ = item(st, 0)
        return bi, jnp.maximum(i * hb - 1, 0), 0

    def xx_map(st):
        bi, i, _ = item(st, 0)
        return bi, jnp.minimum((i + 1) * hb, last_hb), 0

    def gate(lag):
        return lambda st: (0, item(st, lag)[2])

    def up(lag):
        return lambda st: (0, nf + item(st, lag)[2])

    def wo_map(st):
        return item(st, 2)[2], 0

    def y_map(st):
        bi, i, _ = item(st, 2)
        return bi, i, 0

    kern = functools.partial(_ffn_kernel, t=t, n_tiles=n_tiles, nf=nf, n_steps=n_steps)
    return pl.pallas_call(
        kern,
        grid=(n_steps + 2,),
        in_specs=[
            pl.BlockSpec((1, halo, d), xp_map),
            pl.BlockSpec((1, t, d), x_map),
            pl.BlockSpec((1, halo, d), xx_map),
            pl.BlockSpec((1, d), lambda st: (0, 0)),
            pl.BlockSpec((1, d, tf), lambda st: (item(st, 0)[2], 0, 0)),
            pl.BlockSpec((1, d, tf), lambda st: (nf + item(st, 0)[2], 0, 0)),
            pl.BlockSpec((3, tf), gate(1)),
            pl.BlockSpec((3, tf), up(1)),
            pl.BlockSpec((1, tf), gate(1)),
            pl.BlockSpec((1, tf), up(1)),
            pl.BlockSpec((tf, d), wo_map),
        ],
        out_specs=pl.BlockSpec((1, t, d), y_map),
        out_shape=jax.ShapeDtypeStruct((b, s, d), F32),
        scratch_shapes=[pltpu.VMEM((t + 2 * halo, d), BF16)]
        + [pltpu.VMEM((t, tf), BF16)] * 2
        + [pltpu.VMEM((t + 2 * halo, tf), F32)] * 4,
        compiler_params=_params(("arbitrary",)),
        name="ffn",
    )(x, x, x, g, w_in_tiles, w_in_tiles, conv_w, conv_w, conv_b, conv_b, w_out)


def _final_norm_kernel(x_ref, g_ref, y_ref):
    y_ref[0] = _rms(x_ref[0], g_ref[...])


def _final_norm(x, g, *, t=1024):
    b, s, d = x.shape
    return pl.pallas_call(
        _final_norm_kernel,
        grid=(b, s // t),
        in_specs=[pl.BlockSpec((1, t, d), lambda bi, i: (bi, i, 0)),
                  pl.BlockSpec((1, d), lambda bi, i: (0, 0))],
        out_specs=pl.BlockSpec((1, t, d), lambda bi, i: (bi, i, 0)),
        out_shape=jax.ShapeDtypeStruct((b, s, d), F32),
        compiler_params=_params(("parallel", "parallel")),
        name="final_norm",
    )(x, g)


def _rope_tables(seq, dim):
    pos = jnp.arange(seq, dtype=F32)
    inv = 1.0 / (ROPE_THETA ** (jnp.arange(0, dim, 2, dtype=F32) / dim))
    ang = pos[:, None] * inv[None, :]
    return jnp.cos(ang), jnp.sin(ang)


def _prep_tables(seq):
    cos_a, sin_a = _rope_tables(seq, A_HEAD_DIM)
    cos2 = jnp.concatenate([cos_a, cos_a], axis=1)
    sin2 = jnp.concatenate([-sin_a, sin_a], axis=1)
    cos_b, sin_b = _rope_tables(seq, ROPE_DIM)
    half = ROPE_DIM // 2
    z = jnp.zeros((seq, half), F32)
    zz = jnp.zeros((seq, LANES - ROPE_DIM), F32)
    c = jnp.concatenate([cos_b, cos_b, zz], axis=1)
    s1 = jnp.concatenate([-sin_b, z, zz], axis=1)
    s2 = jnp.concatenate([z, sin_b, zz], axis=1)
    return (cos2, sin2), (c, s1, s2)


def _tile_cols(w, tn):
    layers, k, n = w.shape
    return w.astype(BF16).reshape(layers, k, n // tn, tn).transpose(0, 2, 1, 3)


def _prep_b_weights(w_in, w_q_up, w_kv_up):
    pad = jnp.zeros((w_in.shape[0], LANES - ROPE_DIM), w_in.dtype)
    w_in_p = jnp.concatenate([w_in, pad], axis=1).astype(BF16)
    k = w_q_up.shape[0]
    wq = w_q_up.reshape(k, B_HEADS, NOPE_DIM + ROPE_DIM)
    wq = jnp.concatenate(
        [wq, jnp.zeros((k, B_HEADS, 2 * LANES - NOPE_DIM - ROPE_DIM), wq.dtype)], axis=2)
    wq = wq.reshape(k, B_HEADS * 2 * LANES).astype(BF16)
    wkv = w_kv_up.reshape(k, B_HEADS, NOPE_DIM + V_DIM)
    wk = wkv[:, :, :NOPE_DIM].reshape(k, B_HEADS * NOPE_DIM).astype(BF16)
    wvt = wkv[:, :, NOPE_DIM:].reshape(k, B_HEADS * V_DIM).T.astype(BF16)
    return w_in_p, wq, wk, wvt


def _trunk(x, p, tabs_a, tabs_b):
    depth = p["norm_mix"].shape[0]
    for i in range(depth):
        j = i // 2
        g_mix = p["norm_mix"][i][None, :]
        if i % 2 == 0:
            qkv = _qkv_a(x, g_mix, p["a_w_qkv"][j], *tabs_a)
            o = _win_attn(qkv, p["a_sink2"][j])
            x = _out_proj(o, p["a_w_o"][j], x)
        else:
            w_in_p, wq, wk, wvt = p["b_w"][j]
            cq, ckv, kr = _in_b(x, g_mix, w_in_p, p["b_q_norm"][j][None, :],
                                p["b_kv_norm"][j][None, :], *tabs_b)
            q = _proj_q(cq, wq, tabs_b)
            kn, vt = _proj_kv(ckv, wk, wvt)
            o = _mla_attn(q, kn, kr, vt)
            x = _out_proj(o, p["b_w_o"][j], x)
        x = _ffn(x, p["norm_ffn"][i][None, :], p["f_w_in"][i], p["f_conv_w"][i],
                 p["f_conv_b"][i][None, :], p["f_w_out"][i])
    return _final_norm(x, p["norm_final"][None, :])


def kernel(x_prompt, x_sample, norm_mix, norm_ffn, norm_final, a_w_qkv, a_w_o, a_sink,
           b_w_in, b_q_norm, b_w_q_up, b_kv_norm, b_w_kv_up, b_w_o,
           f_w_in, f_conv_w, f_conv_b, f_w_out):
    p = {
        "norm_mix": norm_mix, "norm_ffn": norm_ffn, "norm_final": norm_final,
        "a_w_qkv": a_w_qkv.astype(BF16), "a_w_o": a_w_o.astype(BF16),
        "a_sink2": a_sink * LOG2E,
        "b_q_norm": b_q_norm, "b_kv_norm": b_kv_norm, "b_w_o": b_w_o.astype(BF16),
        "b_w": [_prep_b_weights(b_w_in[j], b_w_q_up[j], b_w_kv_up[j])
                for j in range(b_w_in.shape[0])],
        "f_w_in": _tile_cols(f_w_in, FFN_TF), "f_conv_w": f_conv_w, "f_conv_b": f_conv_b,
        "f_w_out": f_w_out.astype(BF16),
    }
    outs = []
    for x in (x_prompt, x_sample):
        tabs_a, tabs_b = _prep_tables(x.shape[1])
        outs.append(_trunk(x, p, tabs_a, tabs_b))
    return tuple(outs)
```

```python
import functools
import math

import jax
import jax.numpy as jnp
from jax import lax
from jax.experimental import pallas as pl
from jax.experimental.pallas import tpu as pltpu

F32 = jnp.float32
BF16 = jnp.bfloat16

EPS = 1e-6
ROPE_THETA = 10000.0
LANES = 128
BF16_ROWS = 16
WINDOW = 128
A_HEADS = 16
A_KV_HEADS = 4
A_GROUP = A_HEADS // A_KV_HEADS
A_HEAD_DIM = 128
B_HEADS = 16
Q_LORA = 512
KV_LORA = 512
NOPE_DIM = 128
ROPE_DIM = 64
V_DIM = 128
LOG2E = math.log2(math.e)
NEG_BIG = -1e30
VMEM_LIMIT = 56 * 1024 * 1024


def _params(sem):
    return pltpu.CompilerParams(dimension_semantics=sem, vmem_limit_bytes=VMEM_LIMIT)


def _rms(x, g):
    return x * lax.rsqrt(jnp.mean(x * x, axis=-1, keepdims=True) + EPS) * g


def _rope64(x, c, s1, s2):
    return x * c + pltpu.roll(x, 96, 1) * s1 + pltpu.roll(x, 32, 1) * s2


def _qkv_a_kernel(x_ref, g_ref, w_ref, cos_ref, sin_ref, o_ref, xn_ref, *,
                  tn, q_cols, rope_cols, q_scale):
    xn_ref[...] = _rms(x_ref[0], g_ref[...]).astype(BF16)
    cos = cos_ref[...]
    sin = sin_ref[...]
    for j in range(w_ref.shape[1] // tn):
        acc = jnp.dot(xn_ref[...], w_ref[:, j * tn:(j + 1) * tn],
                      preferred_element_type=F32)
        for h in range(tn // LANES):
            col = j * tn + h * LANES
            ch = acc[:, h * LANES:(h + 1) * LANES]
            if col < rope_cols:
                ch = ch * cos + pltpu.roll(ch, LANES // 2, 1) * sin
            if col < q_cols:
                ch = ch * q_scale
            o_ref[0, :, col:col + LANES] = ch.astype(BF16)


def _qkv_a(x, g, w, cos2, sin2, *, t=512, tn=512):
    b, s, d = x.shape
    n = w.shape[1]
    q_cols = A_HEADS * A_HEAD_DIM
    k_cols = A_KV_HEADS * A_HEAD_DIM
    kern = functools.partial(
        _qkv_a_kernel, tn=tn, q_cols=q_cols, rope_cols=q_cols + k_cols,
        q_scale=A_HEAD_DIM ** -0.5 * LOG2E)
    return pl.pallas_call(
        kern,
        grid=(b, s // t),
        in_specs=[
            pl.BlockSpec((1, t, d), lambda bi, i: (bi, i, 0)),
            pl.BlockSpec((1, d), lambda bi, i: (0, 0)),
            pl.BlockSpec((d, n), lambda bi, i: (0, 0)),
            pl.BlockSpec((t, LANES), lambda bi, i: (i, 0)),
            pl.BlockSpec((t, LANES), lambda bi, i: (i, 0)),
        ],
        out_specs=pl.BlockSpec((1, t, n), lambda bi, i: (bi, i, 0)),
        out_shape=jax.ShapeDtypeStruct((b, s, n), BF16),
        scratch_shapes=[pltpu.VMEM((t, d), BF16)],
        compiler_params=_params(("parallel", "parallel")),
        name="qkv_a",
    )(x, g, w, cos2, sin2)


def _win_attn_kernel(sink_ref, q_ref, km_ref, kp_ref, kx_ref, vm_ref, vp_ref, vx_ref,
                     o_ref, kbuf, vbuf, *, tq, n_tiles):
    i = pl.program_id(1)
    blk = WINDOW
    span = 3 * blk
    kbuf[0:blk, :] = kp_ref[0]
    kbuf[blk:blk + tq, :] = km_ref[0]
    kbuf[blk + tq:, :] = kx_ref[0]
    vbuf[0:blk, :] = vp_ref[0]
    vbuf[blk:blk + tq, :] = vm_ref[0]
    vbuf[blk + tq:, :] = vx_ref[0]

    rows = A_GROUP * blk
    a_idx = lax.broadcasted_iota(jnp.int32, (rows, span), 0) % blk
    c_idx = lax.broadcasted_iota(jnp.int32, (rows, span), 1)
    rel = c_idx - a_idx
    band = (rel >= 0) & (rel <= 2 * WINDOW)
    grp = lax.broadcasted_iota(jnp.int32, (rows, 1), 0) // blk
    lo = jnp.where(i == 0, blk, 0)
    hi = jnp.where(i == n_tiles - 1, 2 * blk, span)

    n_sub = tq // blk
    for r in range(n_sub):
        valid = band
        if r == 0:
            valid = valid & (c_idx >= lo)
        if r == n_sub - 1:
            valid = valid & (c_idx < hi)
        for kh in range(A_KV_HEADS):
            qg = jnp.concatenate(
                [q_ref[0, r * blk:(r + 1) * blk,
                       (kh * A_GROUP + g) * A_HEAD_DIM:(kh * A_GROUP + g + 1) * A_HEAD_DIM]
                 for g in range(A_GROUP)], axis=0)
            kw = kbuf[r * blk:r * blk + span, kh * A_HEAD_DIM:(kh + 1) * A_HEAD_DIM]
            vw = vbuf[r * blk:r * blk + span, kh * A_HEAD_DIM:(kh + 1) * A_HEAD_DIM]
            s = lax.dot_general(qg, kw, (((1,), (1,)), ((), ())),
                                preferred_element_type=F32)
            s = jnp.where(valid, s, NEG_BIG)
            sk = jnp.zeros((rows, 1), F32)
            for g in range(A_GROUP):
                sk = jnp.where(grp == g, sink_ref[kh * A_GROUP + g], sk)
            m = jnp.maximum(jnp.max(s, axis=1, keepdims=True), sk)
            p = jnp.exp2(s - m)
            den = jnp.sum(p, axis=1, keepdims=True) + jnp.exp2(sk - m)
            o = jnp.dot(p.astype(BF16), vw, preferred_element_type=F32) / den
            for g in range(A_GROUP):
                col = (kh * A_GROUP + g) * A_HEAD_DIM
                o_ref[0, r * blk:(r + 1) * blk, col:col + A_HEAD_DIM] = (
                    o[g * blk:(g + 1) * blk].astype(BF16))


def _win_attn(qkv, sink2, *, tq=512):
    b, s, _ = qkv.shape
    d = A_HEADS * A_HEAD_DIM
    kd = A_KV_HEADS * A_HEAD_DIM
    blk = WINDOW
    n_tiles = s // tq
    sub = tq // blk
    nb = s // blk
    k_col = d // kd
    v_col = k_col + 1

    def main(col):
        return pl.BlockSpec((1, tq, kd), lambda bi, i: (bi, i, col))

    def prev(col):
        return pl.BlockSpec((1, blk, kd), lambda bi, i: (bi, jnp.maximum(i * sub - 1, 0), col))

    def nxt(col):
        return pl.BlockSpec((1, blk, kd), lambda bi, i: (bi, jnp.minimum((i + 1) * sub, nb - 1), col))

    kern = functools.partial(_win_attn_kernel, tq=tq, n_tiles=n_tiles)
    return pl.pallas_call(
        kern,
        grid=(b, n_tiles),
        in_specs=[
            pl.BlockSpec(memory_space=pltpu.SMEM),
            pl.BlockSpec((1, tq, d), lambda bi, i: (bi, i, 0)),
            main(k_col), prev(k_col), nxt(k_col),
            main(v_col), prev(v_col), nxt(v_col),
        ],
        out_specs=pl.BlockSpec((1, tq, d), lambda bi, i: (bi, i, 0)),
        out_shape=jax.ShapeDtypeStruct((b, s, d), BF16),
        scratch_shapes=[pltpu.VMEM((tq + 2 * blk, kd), BF16),
                        pltpu.VMEM((tq + 2 * blk, kd), BF16)],
        compiler_params=_params(("parallel", "parallel")),
        name="win_attn",
    )(sink2, qkv, qkv, qkv, qkv, qkv, qkv, qkv)


def _in_b_kernel(x_ref, g_ref, w_ref, qn_ref, kvn_ref, c_ref, s1_ref, s2_ref,
                 cq_ref, ckv_ref, kr_ref):
    xn = _rms(x_ref[0], g_ref[...]).astype(BF16)
    hq = jnp.dot(xn, w_ref[:, :Q_LORA], preferred_element_type=F32)
    hkv = jnp.dot(xn, w_ref[:, Q_LORA:Q_LORA + KV_LORA], preferred_element_type=F32)
    cq_ref[0] = _rms(hq, qn_ref[...]).astype(BF16)
    hr = jnp.dot(xn, w_ref[:, Q_LORA + KV_LORA:], preferred_element_type=F32)
    ckv_ref[0] = _rms(hkv, kvn_ref[...]).astype(BF16)
    kr_ref[0] = _rope64(hr, c_ref[...], s1_ref[...], s2_ref[...]).astype(BF16)


def _in_b(x, g, w, qn, kvn, c, s1, s2, *, t=512):
    b, s, d = x.shape
    n = w.shape[1]
    row = lambda bi, i: (bi, i, 0)
    fixed = lambda bi, i: (0, 0)
    tab = pl.BlockSpec((t, LANES), lambda bi, i: (i, 0))
    return pl.pallas_call(
        _in_b_kernel,
        grid=(b, s // t),
        in_specs=[
            pl.BlockSpec((1, t, d), row),
            pl.BlockSpec((1, d), fixed),
            pl.BlockSpec((d, n), fixed),
            pl.BlockSpec((1, Q_LORA), fixed),
            pl.BlockSpec((1, KV_LORA), fixed),
            tab, tab, tab,
        ],
        out_specs=[pl.BlockSpec((1, t, Q_LORA), row),
                   pl.BlockSpec((1, t, KV_LORA), row),
                   pl.BlockSpec((1, t, LANES), row)],
        out_shape=[jax.ShapeDtypeStruct((b, s, Q_LORA), BF16),
                   jax.ShapeDtypeStruct((b, s, KV_LORA), BF16),
                   jax.ShapeDtypeStruct((b, s, LANES), BF16)],
        compiler_params=_params(("parallel", "parallel")),
        name="in_b",
    )(x, g, w, qn, kvn, c, s1, s2)


def _proj_q_kernel(x_ref, w_ref, c_ref, s1_ref, s2_ref, o_ref, *, tn, scale):
    c = c_ref[...] * scale
    s1 = s1_ref[...] * scale
    s2 = s2_ref[...] * scale
    hw = 2 * LANES
    for j in range(w_ref.shape[1] // tn):
        acc = jnp.dot(x_ref[0], w_ref[:, j * tn:(j + 1) * tn], preferred_element_type=F32)
        for h in range(tn // hw):
            col = j * tn + h * hw
            o_ref[0, :, col:col + LANES] = (
                acc[:, h * hw:h * hw + LANES] * scale).astype(BF16)
            o_ref[0, :, col + LANES:col + hw] = _rope64(
                acc[:, h * hw + LANES:(h + 1) * hw], c, s1, s2).astype(BF16)


def _proj_q(x, w, tables, *, t=512, tn=1024):
    b, s, k = x.shape
    n = w.shape[1]
    kern = functools.partial(_proj_q_kernel, tn=tn,
                             scale=(NOPE_DIM + ROPE_DIM) ** -0.5 * LOG2E)
    tab = pl.BlockSpec((t, LANES), lambda bi, i: (i, 0))
    return pl.pallas_call(
        kern,
        grid=(b, s // t),
        in_specs=[pl.BlockSpec((1, t, k), lambda bi, i: (bi, i, 0)),
                  pl.BlockSpec((k, n), lambda bi, i: (0, 0)),
                  tab, tab, tab],
        out_specs=pl.BlockSpec((1, t, n), lambda bi, i: (bi, i, 0)),
        out_shape=jax.ShapeDtypeStruct((b, s, n), BF16),
        compiler_params=_params(("parallel", "parallel")),
        name="proj_q",
    )(x, w, *tables)


def _proj_kv_kernel(x_ref, wk_ref, wvt_ref, kn_ref, vt_ref, *, tn):
    x = x_ref[0]
    for j in range(wk_ref.shape[1] // tn):
        cols = slice(j * tn, (j + 1) * tn)
        kn_ref[0, :, cols] = jnp.dot(x, wk_ref[:, cols],
                                     preferred_element_type=F32).astype(BF16)
        vt_ref[0, cols, :] = lax.dot_general(wvt_ref[cols, :], x, (((1,), (1,)), ((), ())),
                                             preferred_element_type=F32).astype(BF16)


def _proj_kv(x, wk, wvt, *, t=512, tn=512):
    b, s, k = x.shape
    n = wk.shape[1]
    return pl.pallas_call(
        functools.partial(_proj_kv_kernel, tn=tn),
        grid=(b, s // t),
        in_specs=[pl.BlockSpec((1, t, k), lambda bi, i: (bi, i, 0)),
                  pl.BlockSpec((k, n), lambda bi, i: (0, 0)),
                  pl.BlockSpec((n, k), lambda bi, i: (0, 0))],
        out_specs=[pl.BlockSpec((1, t, n), lambda bi, i: (bi, i, 0)),
                   pl.BlockSpec((1, n, t), lambda bi, i: (bi, 0, i))],
        out_shape=[jax.ShapeDtypeStruct((b, s, n), BF16),
                   jax.ShapeDtypeStruct((b, n, s), BF16)],
        compiler_params=_params(("parallel", "parallel")),
        name="proj_kv",
    )(x, wk, wvt)


def _mla_kernel(q_ref, kn_ref, kr_ref, vt_ref, o_ref, kcat_ref, st_ref, m_ref, *,
                n_i, kc, n_steps):
    t = pl.program_id(0)
    s_len, tq = st_ref.shape
    sub8 = 8

    @pl.when(t == 0)
    def _():
        st_ref[...] = jnp.zeros(st_ref.shape, F32)
        m_ref[...] = jnp.zeros(m_ref.shape, F32)

    @pl.when((t % n_i == 0) & (t < n_steps))
    def _():
        kcat_ref[:, :LANES] = kn_ref[0]
        kcat_ref[:, LANES:] = kr_ref[0]

    q = q_ref[0]
    m_old = m_ref[...]
    m_new = None
    den8 = None
    ot = None
    for c in range(s_len // kc):
        rows = slice(c * kc, (c + 1) * kc)
        p = jnp.exp2(st_ref[rows, :] - m_old)
        part = jnp.sum(p.reshape(kc // sub8, sub8, tq), axis=0)
        den8 = part if den8 is None else den8 + part
        pv = jnp.dot(vt_ref[0, :, rows], p.astype(BF16), preferred_element_type=F32)
        ot = pv if ot is None else ot + pv
        s_new = lax.dot_general(kcat_ref[rows, :], q, (((1,), (1,)), ((), ())),
                                preferred_element_type=F32)
        st_ref[rows, :] = s_new
        part = jnp.max(s_new.reshape(kc // sub8, sub8, tq), axis=0)
        m_new = part if m_new is None else jnp.maximum(m_new, part)
    m_ref[...] = jnp.max(m_new, axis=0, keepdims=True)
    den = jnp.sum(den8, axis=0, keepdims=True)
    o_ref[0] = (ot / den).T.astype(BF16)


def _mla_attn(q, kn, kr, vt, *, tq=1024, kc=512):
    b, s, _ = q.shape
    qw = 2 * LANES
    n_i = s // tq
    n_steps = b * B_HEADS * n_i

    def item(t, lag):
        t = jnp.clip(t - lag, 0, n_steps - 1)
        bh = t // n_i
        return bh // B_HEADS, bh % B_HEADS, t % n_i

    def q_map(t):
        bi, h, i = item(t, 0)
        return bi, i, h

    def kn_map(t):
        bi, h, _ = item(t, 0)
        return bi, 0, h

    def kr_map(t):
        bi, _, _ = item(t, 0)
        return bi, 0, 0

    def vt_map(t):
        bi, h, _ = item(t, 1)
        return bi, h, 0

    def o_map(t):
        bi, h, i = item(t, 1)
        return bi, i, h

    kern = functools.partial(_mla_kernel, n_i=n_i, kc=kc, n_steps=n_steps)
    return pl.pallas_call(
        kern,
        grid=(n_steps + 1,),
        in_specs=[
            pl.BlockSpec((1, tq, qw), q_map),
            pl.BlockSpec((1, s, NOPE_DIM), kn_map),
            pl.BlockSpec((1, s, LANES), kr_map),
            pl.BlockSpec((1, V_DIM, s), vt_map),
        ],
        out_specs=pl.BlockSpec((1, tq, V_DIM), o_map),
        out_shape=jax.ShapeDtypeStruct((b, s, B_HEADS * V_DIM), BF16),
        scratch_shapes=[pltpu.VMEM((s, qw), BF16),
                        pltpu.VMEM((s, tq), F32),
                        pltpu.VMEM((1, tq), F32)],
        compiler_params=_params(("arbitrary",)),
        name="mla_attn",
    )(q, kn, kr, vt)


def _out_proj_kernel(o_ref, w_ref, x_ref, y_ref, *, tn):
    for j in range(w_ref.shape[1] // tn):
        cols = slice(j * tn, (j + 1) * tn)
        y_ref[0, :, cols] = x_ref[0, :, cols] + jnp.dot(
            o_ref[0], w_ref[:, cols], preferred_element_type=F32)


def _out_proj(o, w, x, *, t=512, tn=512):
    b, s, k = o.shape
    n = w.shape[1]
    return pl.pallas_call(
        functools.partial(_out_proj_kernel, tn=tn),
        grid=(b, s // t),
        in_specs=[pl.BlockSpec((1, t, k), lambda bi, i: (bi, i, 0)),
                  pl.BlockSpec((k, n), lambda bi, i: (0, 0)),
                  pl.BlockSpec((1, t, n), lambda bi, i: (bi, i, 0))],
        out_specs=pl.BlockSpec((1, t, n), lambda bi, i: (bi, i, 0)),
        out_shape=jax.ShapeDtypeStruct((b, s, n), F32),
        compiler_params=_params(("parallel", "parallel")),
        name="out_proj",
    )(o, w, x)


def _ffn_kernel(xp_ref, x_ref, xx_ref, g_ref, wg_ref, wu_ref, cwg_ref, cwu_ref,
                cbg_ref, cbu_ref, wo_ref, y_ref, xn_ref, a0_ref, a1_ref, hg0_ref, hg1_ref,
                hu0_ref, hu1_ref, *, t, n_tiles, nf, n_steps):
    k = pl.program_id(0)
    f = k % nf
    i = (k // nf) % n_tiles
    halo = BF16_ROWS

    a_refs = (a0_ref, a1_ref)
    hg_refs = (hg0_ref, hg1_ref)
    hu_refs = (hu0_ref, hu1_ref)

    @pl.when(k == 0)
    def _():
        for ref in a_refs + hg_refs + hu_refs + (y_ref,):
            ref[...] = jnp.zeros(ref.shape, ref.dtype)

    @pl.when((f == 0) & (k < n_steps))
    def _():
        g = g_ref[...]
        prev = jnp.where(i > 0, _rms(xp_ref[0], g), 0.0)
        nxt = jnp.where(i < n_tiles - 1, _rms(xx_ref[0], g), 0.0)
        xn_ref[0:halo, :] = prev.astype(BF16)
        xn_ref[halo:halo + t, :] = _rms(x_ref[0], g).astype(BF16)
        xn_ref[halo + t:, :] = nxt.astype(BF16)

    @pl.when(f == 2)
    def _():
        y_ref[0] = x_ref[0]

    tf = a0_ref.shape[1]
    d = y_ref.shape[2]
    rb = 64
    blocks = [(r, c) for c in range(tf // LANES) for r in range(t // rb)]
    m_rows = t + 2 * halo
    up_rows = m_rows // (3 if m_rows % (3 * BF16_ROWS) == 0 else 2)
    down_rows = min(t, 512)
    down_cols = 512

    def conv(h_ref, r, cols, cw_ref, cb_ref):
        cw = cw_ref[:, cols]
        lo = halo + r * rb
        blk = h_ref[lo - 8:lo + rb + 8, cols]
        return (blk[7:7 + rb] * cw[0:1] + blk[8:8 + rb] * cw[1:2]
                + blk[9:9 + rb] * cw[2:3] + cb_ref[:, cols])

    def stages(p):
        def gate_block(r, c):
            cols = slice(c * LANES, (c + 1) * LANES)
            gc = conv(hg_refs[1 - p], r, cols, cwg_ref, cbg_ref)
            uc = conv(hu_refs[1 - p], r, cols, cwu_ref, cbu_ref)
            a_refs[1 - p][r * rb:(r + 1) * rb, cols] = (
                gc * jax.nn.sigmoid(gc) * uc).astype(BF16)

        def up_piece(h_ref, w_ref, lo):
            h_ref[lo:lo + up_rows, :] = jnp.dot(xn_ref[lo:lo + up_rows, :], w_ref[0],
                                                preferred_element_type=F32)

        def down_piece(r0, c0):
            rows = slice(r0, r0 + down_rows)
            cols = slice(c0, c0 + down_cols)
            y_ref[0, rows, cols] += jnp.dot(a_refs[p][rows, :], wo_ref[:, cols],
                                            preferred_element_type=F32)

        pieces = [(functools.partial(up_piece, h_ref, w_ref, lo), up_rows * d)
                  for lo in range(0, m_rows, up_rows)
                  for h_ref, w_ref in ((hg_refs[p], wg_ref), (hu_refs[p], wu_ref))]
        pieces += [(functools.partial(down_piece, r0, c0), down_rows * down_cols)
                   for r0 in range(0, t, down_rows) for c0 in range(0, d, down_cols)]
        total = sum(w for _, w in pieces)
        done = 0
        acc = 0
        for piece, w in pieces:
            piece()
            acc += w
            upto = len(blocks) * acc // total
            for r, c in blocks[done:upto]:
                gate_block(r, c)
            done = upto

    @pl.when(k % 2 == 0)
    def _():
        stages(0)

    @pl.when(k % 2 == 1)
    def _():
        stages(1)


FFN_TF = 256


def _ffn(x, g, w_in_tiles, conv_w, conv_b, w_out, *, t=1024):
    b, s, d = x.shape
    d_ff = w_out.shape[0]
    tf = w_in_tiles.shape[2]
    nf = d_ff // tf
    assert nf >= 3 and w_in_tiles.shape[0] == 2 * nf
    n_tiles = s // t
    n_steps = b * n_tiles * nf
    halo = BF16_ROWS
    hb = t // halo
    last_hb = s // halo - 1

    def item(st, lag):
        st = jnp.clip(st - lag, 0, n_steps - 1)
        tile = st // nf
        return tile // n_tiles, tile % n_tiles, st % nf

    def x_map(st):
        bi, i, _ = item(st, 0)
        return bi, i, 0

    def xp_map(st):
        bi, i, _ = item(st, 0)
        return bi, jnp.maximum(i * hb - 1, 0), 0

    def xx_map(st):
        bi, i, _ = item(st, 0)
        return bi, jnp.minimum((i + 1) * hb, last_hb), 0

    def gate(lag):
        return lambda st: (0, item(st, lag)[2])

    def up(lag):
        return lambda st: (0, nf + item(st, lag)[2])

    def wo_map(st):
        return item(st, 2)[2], 0

    def y_map(st):
        bi, i, _ = item(st, 2)
        return bi, i, 0

    kern = functools.partial(_ffn_kernel, t=t, n_tiles=n_tiles, nf=nf, n_steps=n_steps)
    return pl.pallas_call(
        kern,
        grid=(n_steps + 2,),
        in_specs=[
            pl.BlockSpec((1, halo, d), xp_map),
            pl.BlockSpec((1, t, d), x_map),
            pl.BlockSpec((1, halo, d), xx_map),
            pl.BlockSpec((1, d), lambda st: (0, 0)),
            pl.BlockSpec((1, d, tf), lambda st: (item(st, 0)[2], 0, 0)),
            pl.BlockSpec((1, d, tf), lambda st: (nf + item(st, 0)[2], 0, 0)),
            pl.BlockSpec((3, tf), gate(1)),
            pl.BlockSpec((3, tf), up(1)),
            pl.BlockSpec((1, tf), gate(1)),
            pl.BlockSpec((1, tf), up(1)),
            pl.BlockSpec((tf, d), wo_map),
        ],
        out_specs=pl.BlockSpec((1, t, d), y_map),
        out_shape=jax.ShapeDtypeStruct((b, s, d), F32),
        scratch_shapes=[pltpu.VMEM((t + 2 * halo, d), BF16)]
        + [pltpu.VMEM((t, tf), BF16)] * 2
        + [pltpu.VMEM((t + 2 * halo, tf), F32)] * 4,
        compiler_params=_params(("arbitrary",)),
        name="ffn",
    )(x, x, x, g, w_in_tiles, w_in_tiles, conv_w, conv_w, conv_b, conv_b, w_out)


def _final_norm_kernel(x_ref, g_ref, y_ref):
    y_ref[0] = _rms(x_ref[0], g_ref[...])


def _final_norm(x, g, *, t=1024):
    b, s, d = x.shape
    return pl.pallas_call(
        _final_norm_kernel,
        grid=(b, s // t),
        in_specs=[pl.BlockSpec((1, t, d), lambda bi, i: (bi, i, 0)),
                  pl.BlockSpec((1, d), lambda bi, i: (0, 0))],
        out_specs=pl.BlockSpec((1, t, d), lambda bi, i: (bi, i, 0)),
        out_shape=jax.ShapeDtypeStruct((b, s, d), F32),
        compiler_params=_params(("parallel", "parallel")),
        name="final_norm",
    )(x, g)


def _rope_tables(seq, dim):
    pos = jnp.arange(seq, dtype=F32)
    inv = 1.0 / (ROPE_THETA ** (jnp.arange(0, dim, 2, dtype=F32) / dim))
    ang = pos[:, None] * inv[None, :]
    return jnp.cos(ang), jnp.sin(ang)


def _prep_tables(seq):
    cos_a, sin_a = _rope_tables(seq, A_HEAD_DIM)
    cos2 = jnp.concatenate([cos_a, cos_a], axis=1)
    sin2 = jnp.concatenate([-sin_a, sin_a], axis=1)
    cos_b, sin_b = _rope_tables(seq, ROPE_DIM)
    half = ROPE_DIM // 2
    z = jnp.zeros((seq, half), F32)
    zz = jnp.zeros((seq, LANES - ROPE_DIM), F32)
    c = jnp.concatenate([cos_b, cos_b, zz], axis=1)
    s1 = jnp.concatenate([-sin_b, z, zz], axis=1)
    s2 = jnp.concatenate([z, sin_b, zz], axis=1)
    return (cos2, sin2), (c, s1, s2)


def _tile_cols(w, tn):
    layers, k, n = w.shape
    return w.reshape(layers, k, n // tn, tn).transpose(0, 2, 1, 3).astype(BF16)


def _prep_b_weights(w_in, w_q_up, w_kv_up):
    pad = jnp.zeros((w_in.shape[0], LANES - ROPE_DIM), w_in.dtype)
    w_in_p = jnp.concatenate([w_in, pad], axis=1).astype(BF16)
    k = w_q_up.shape[0]
    wq = w_q_up.reshape(k, B_HEADS, NOPE_DIM + ROPE_DIM)
    wq = jnp.concatenate(
        [wq, jnp.zeros((k, B_HEADS, 2 * LANES - NOPE_DIM - ROPE_DIM), wq.dtype)], axis=2)
    wq = wq.reshape(k, B_HEADS * 2 * LANES).astype(BF16)
    wkv = w_kv_up.reshape(k, B_HEADS, NOPE_DIM + V_DIM)
    wk = wkv[:, :, :NOPE_DIM].reshape(k, B_HEADS * NOPE_DIM).astype(BF16)
    wvt = wkv[:, :, NOPE_DIM:].reshape(k, B_HEADS * V_DIM).T.astype(BF16)
    return w_in_p, wq, wk, wvt


def _trunk(x, p, tabs_a, tabs_b):
    depth = p["norm_mix"].shape[0]
    for i in range(depth):
        j = i // 2
        g_mix = p["norm_mix"][i][None, :]
        if i % 2 == 0:
            qkv = _qkv_a(x, g_mix, p["a_w_qkv"][j], *tabs_a)
            o = _win_attn(qkv, p["a_sink2"][j])
            x = _out_proj(o, p["a_w_o"][j], x)
        else:
            w_in_p, wq, wk, wvt = p["b_w"][j]
            cq, ckv, kr = _in_b(x, g_mix, w_in_p, p["b_q_norm"][j][None, :],
                                p["b_kv_norm"][j][None, :], *tabs_b)
            q = _proj_q(cq, wq, tabs_b)
            kn, vt = _proj_kv(ckv, wk, wvt)
            o = _mla_attn(q, kn, kr, vt)
            x = _out_proj(o, p["b_w_o"][j], x)
        x = _ffn(x, p["norm_ffn"][i][None, :], p["f_w_in"][i], p["f_conv_w"][i],
                 p["f_conv_b"][i][None, :], p["f_w_out"][i])
    return _final_norm(x, p["norm_final"][None, :])


def kernel(x_prompt, x_sample, norm_mix, norm_ffn, norm_final, a_w_qkv, a_w_o, a_sink,
           b_w_in, b_q_norm, b_w_q_up, b_kv_norm, b_w_kv_up, b_w_o,
           f_w_in, f_conv_w, f_conv_b, f_w_out):
    p = {
        "norm_mix": norm_mix, "norm_ffn": norm_ffn, "norm_final": norm_final,
        "a_w_qkv": a_w_qkv.astype(BF16), "a_w_o": a_w_o.astype(BF16),
        "a_sink2": a_sink * LOG2E,
        "b_q_norm": b_q_norm, "b_kv_norm": b_kv_norm, "b_w_o": b_w_o.astype(BF16),
        "b_w": [_prep_b_weights(b_w_in[j], b_w_q_up[j], b_w_kv_up[j])
                for j in range(b_w_in.shape[0])],
        "f_w_in": _tile_cols(f_w_in, FFN_TF), "f_conv_w": f_conv_w, "f_conv_b": f_conv_b,
        "f_w_out": f_w_out.astype(BF16),
    }
    outs = []
    for x in (x_prompt, x_sample):
        tabs_a, tabs_b = _prep_tables(x.shape[1])
        outs.append(_trunk(x, p, tabs_a, tabs_b))
    return tuple(outs)
```

```python
import functools
import math

import jax
import jax.numpy as jnp
from jax import lax
from jax.experimental import pallas as pl
from jax.experimental.pallas import tpu as pltpu

F32 = jnp.float32
BF16 = jnp.bfloat16

EPS = 1e-6
ROPE_THETA = 10000.0
LANES = 128
BF16_ROWS = 16
WINDOW = 128
A_HEADS = 16
A_KV_HEADS = 4
A_GROUP = A_HEADS // A_KV_HEADS
A_HEAD_DIM = 128
B_HEADS = 16
Q_LORA = 512
KV_LORA = 512
NOPE_DIM = 128
ROPE_DIM = 64
V_DIM = 128
LOG2E = math.log2(math.e)
NEG_BIG = -1e30
VMEM_LIMIT = 58 * 1024 * 1024


def _params(sem):
    return pltpu.CompilerParams(dimension_semantics=sem, vmem_limit_bytes=VMEM_LIMIT)


def _rms(x, g):
    return x * lax.rsqrt(jnp.mean(x * x, axis=-1, keepdims=True) + EPS) * g


def _rope64(x, c, s1, s2):
    return x * c + pltpu.roll(x, 96, 1) * s1 + pltpu.roll(x, 32, 1) * s2


def _qkv_a_kernel(x_ref, g_ref, w_ref, cos_ref, sin_ref, o_ref, xn_ref, *,
                  tn, q_cols, rope_cols, q_scale):
    xn_ref[...] = _rms(x_ref[0], g_ref[...]).astype(BF16)
    cos = cos_ref[...]
    sin = sin_ref[...]
    for j in range(w_ref.shape[1] // tn):
        acc = jnp.dot(xn_ref[...], w_ref[:, j * tn:(j + 1) * tn],
                      preferred_element_type=F32)
        for h in range(tn // LANES):
            col = j * tn + h * LANES
            ch = acc[:, h * LANES:(h + 1) * LANES]
            if col < rope_cols:
                ch = ch * cos + pltpu.roll(ch, LANES // 2, 1) * sin
            if col < q_cols:
                ch = ch * q_scale
            o_ref[0, :, col:col + LANES] = ch.astype(BF16)


def _qkv_a(x, g, w, cos2, sin2, *, t=512, tn=512):
    b, s, d = x.shape
    n = w.shape[1]
    q_cols = A_HEADS * A_HEAD_DIM
    k_cols = A_KV_HEADS * A_HEAD_DIM
    kern = functools.partial(
        _qkv_a_kernel, tn=tn, q_cols=q_cols, rope_cols=q_cols + k_cols,
        q_scale=A_HEAD_DIM ** -0.5 * LOG2E)
    return pl.pallas_call(
        kern,
        grid=(b, s // t),
        in_specs=[
            pl.BlockSpec((1, t, d), lambda bi, i: (bi, i, 0)),
            pl.BlockSpec((1, d), lambda bi, i: (0, 0)),
            pl.BlockSpec((d, n), lambda bi, i: (0, 0)),
            pl.BlockSpec((t, LANES), lambda bi, i: (i, 0)),
            pl.BlockSpec((t, LANES), lambda bi, i: (i, 0)),
        ],
        out_specs=pl.BlockSpec((1, t, n), lambda bi, i: (bi, i, 0)),
        out_shape=jax.ShapeDtypeStruct((b, s, n), BF16),
        scratch_shapes=[pltpu.VMEM((t, d), BF16)],
        compiler_params=_params(("parallel", "parallel")),
        name="qkv_a",
    )(x, g, w, cos2, sin2)


def _win_attn_kernel(sink_ref, q_ref, km_ref, kp_ref, kx_ref, vm_ref, vp_ref, vx_ref,
                     o_ref, kbuf, vbuf, *, tq, n_tiles):
    i = pl.program_id(1)
    blk = WINDOW
    span = 3 * blk
    kbuf[0:blk, :] = kp_ref[0]
    kbuf[blk:blk + tq, :] = km_ref[0]
    kbuf[blk + tq:, :] = kx_ref[0]
    vbuf[0:blk, :] = vp_ref[0]
    vbuf[blk:blk + tq, :] = vm_ref[0]
    vbuf[blk + tq:, :] = vx_ref[0]

    rows = A_GROUP * blk
    a_idx = lax.broadcasted_iota(jnp.int32, (rows, span), 0) % blk
    c_idx = lax.broadcasted_iota(jnp.int32, (rows, span), 1)
    rel = c_idx - a_idx
    band = (rel >= 0) & (rel <= 2 * WINDOW)
    grp = lax.broadcasted_iota(jnp.int32, (rows, 1), 0) // blk
    lo = jnp.where(i == 0, blk, 0)
    hi = jnp.where(i == n_tiles - 1, 2 * blk, span)

    n_sub = tq // blk
    for r in range(n_sub):
        valid = band
        if r == 0:
            valid = valid & (c_idx >= lo)
        if r == n_sub - 1:
            valid = valid & (c_idx < hi)
        for kh in range(A_KV_HEADS):
            qg = jnp.concatenate(
                [q_ref[0, r * blk:(r + 1) * blk,
                       (kh * A_GROUP + g) * A_HEAD_DIM:(kh * A_GROUP + g + 1) * A_HEAD_DIM]
                 for g in range(A_GROUP)], axis=0)
            kw = kbuf[r * blk:r * blk + span, kh * A_HEAD_DIM:(kh + 1) * A_HEAD_DIM]
            vw = vbuf[r * blk:r * blk + span, kh * A_HEAD_DIM:(kh + 1) * A_HEAD_DIM]
            s = lax.dot_general(qg, kw, (((1,), (1,)), ((), ())),
                                preferred_element_type=F32)
            s = jnp.where(valid, s, NEG_BIG)
            sk = jnp.zeros((rows, 1), F32)
            for g in range(A_GROUP):
                sk = jnp.where(grp == g, sink_ref[kh * A_GROUP + g], sk)
            m = jnp.maximum(jnp.max(s, axis=1, keepdims=True), sk)
            p = jnp.exp2(s - m)
            den = jnp.sum(p, axis=1, keepdims=True) + jnp.exp2(sk - m)
            o = jnp.dot(p.astype(BF16), vw, preferred_element_type=F32) / den
            for g in range(A_GROUP):
                col = (kh * A_GROUP + g) * A_HEAD_DIM
                o_ref[0, r * blk:(r + 1) * blk, col:col + A_HEAD_DIM] = (
                    o[g * blk:(g + 1) * blk].astype(BF16))


def _win_attn(qkv, sink2, *, tq=512):
    b, s, _ = qkv.shape
    d = A_HEADS * A_HEAD_DIM
    kd = A_KV_HEADS * A_HEAD_DIM
    blk = WINDOW
    n_tiles = s // tq
    sub = tq // blk
    nb = s // blk
    k_col = d // kd
    v_col = k_col + 1

    def main(col):
        return pl.BlockSpec((1, tq, kd), lambda bi, i: (bi, i, col))

    def prev(col):
        return pl.BlockSpec((1, blk, kd), lambda bi, i: (bi, jnp.maximum(i * sub - 1, 0), col))

    def nxt(col):
        return pl.BlockSpec((1, blk, kd), lambda bi, i: (bi, jnp.minimum((i + 1) * sub, nb - 1), col))

    kern = functools.partial(_win_attn_kernel, tq=tq, n_tiles=n_tiles)
    return pl.pallas_call(
        kern,
        grid=(b, n_tiles),
        in_specs=[
            pl.BlockSpec(memory_space=pltpu.SMEM),
            pl.BlockSpec((1, tq, d), lambda bi, i: (bi, i, 0)),
            main(k_col), prev(k_col), nxt(k_col),
            main(v_col), prev(v_col), nxt(v_col),
        ],
        out_specs=pl.BlockSpec((1, tq, d), lambda bi, i: (bi, i, 0)),
        out_shape=jax.ShapeDtypeStruct((b, s, d), BF16),
        scratch_shapes=[pltpu.VMEM((tq + 2 * blk, kd), BF16),
                        pltpu.VMEM((tq + 2 * blk, kd), BF16)],
        compiler_params=_params(("parallel", "parallel")),
        name="win_attn",
    )(sink2, qkv, qkv, qkv, qkv, qkv, qkv, qkv)


def _in_b_kernel(x_ref, g_ref, w_ref, qn_ref, kvn_ref, c_ref, s1_ref, s2_ref,
                 cq_ref, ckv_ref, kr_ref):
    xn = _rms(x_ref[0], g_ref[...]).astype(BF16)
    hq = jnp.dot(xn, w_ref[:, :Q_LORA], preferred_element_type=F32)
    hkv = jnp.dot(xn, w_ref[:, Q_LORA:Q_LORA + KV_LORA], preferred_element_type=F32)
    cq_ref[0] = _rms(hq, qn_ref[...]).astype(BF16)
    hr = jnp.dot(xn, w_ref[:, Q_LORA + KV_LORA:], preferred_element_type=F32)
    ckv_ref[0] = _rms(hkv, kvn_ref[...]).astype(BF16)
    kr_ref[0] = _rope64(hr, c_ref[...], s1_ref[...], s2_ref[...]).astype(BF16)


def _in_b(x, g, w, qn, kvn, c, s1, s2, *, t=512):
    b, s, d = x.shape
    n = w.shape[1]
    row = lambda bi, i: (bi, i, 0)
    fixed = lambda bi, i: (0, 0)
    tab = pl.BlockSpec((t, LANES), lambda bi, i: (i, 0))
    return pl.pallas_call(
        _in_b_kernel,
        grid=(b, s // t),
        in_specs=[
            pl.BlockSpec((1, t, d), row),
            pl.BlockSpec((1, d), fixed),
            pl.BlockSpec((d, n), fixed),
            pl.BlockSpec((1, Q_LORA), fixed),
            pl.BlockSpec((1, KV_LORA), fixed),
            tab, tab, tab,
        ],
        out_specs=[pl.BlockSpec((1, t, Q_LORA), row),
                   pl.BlockSpec((1, t, KV_LORA), row),
                   pl.BlockSpec((1, t, LANES), row)],
        out_shape=[jax.ShapeDtypeStruct((b, s, Q_LORA), BF16),
                   jax.ShapeDtypeStruct((b, s, KV_LORA), BF16),
                   jax.ShapeDtypeStruct((b, s, LANES), BF16)],
        compiler_params=_params(("parallel", "parallel")),
        name="in_b",
    )(x, g, w, qn, kvn, c, s1, s2)


def _proj_q_kernel(x_ref, w_ref, c_ref, s1_ref, s2_ref, o_ref, *, tn, scale):
    c = c_ref[...] * scale
    s1 = s1_ref[...] * scale
    s2 = s2_ref[...] * scale
    hw = 2 * LANES
    for j in range(w_ref.shape[1] // tn):
        acc = jnp.dot(x_ref[0], w_ref[:, j * tn:(j + 1) * tn], preferred_element_type=F32)
        for h in range(tn // hw):
            col = j * tn + h * hw
            o_ref[0, :, col:col + LANES] = (
                acc[:, h * hw:h * hw + LANES] * scale).astype(BF16)
            o_ref[0, :, col + LANES:col + hw] = _rope64(
                acc[:, h * hw + LANES:(h + 1) * hw], c, s1, s2).astype(BF16)


def _proj_q(x, w, tables, *, t=512, tn=1024):
    b, s, k = x.shape
    n = w.shape[1]
    kern = functools.partial(_proj_q_kernel, tn=tn,
                             scale=(NOPE_DIM + ROPE_DIM) ** -0.5 * LOG2E)
    tab = pl.BlockSpec((t, LANES), lambda bi, i: (i, 0))
    return pl.pallas_call(
        kern,
        grid=(b, s // t),
        in_specs=[pl.BlockSpec((1, t, k), lambda bi, i: (bi, i, 0)),
                  pl.BlockSpec((k, n), lambda bi, i: (0, 0)),
                  tab, tab, tab],
        out_specs=pl.BlockSpec((1, t, n), lambda bi, i: (bi, i, 0)),
        out_shape=jax.ShapeDtypeStruct((b, s, n), BF16),
        compiler_params=_params(("parallel", "parallel")),
        name="proj_q",
    )(x, w, *tables)


def _proj_kv_kernel(x_ref, wk_ref, wvt_ref, kn_ref, vt_ref, *, tn):
    x = x_ref[0]
    for j in range(wk_ref.shape[1] // tn):
        cols = slice(j * tn, (j + 1) * tn)
        kn_ref[0, :, cols] = jnp.dot(x, wk_ref[:, cols],
                                     preferred_element_type=F32).astype(BF16)
        vt_ref[0, cols, :] = lax.dot_general(wvt_ref[cols, :], x, (((1,), (1,)), ((), ())),
                                             preferred_element_type=F32).astype(BF16)


def _proj_kv(x, wk, wvt, *, t=512, tn=512):
    b, s, k = x.shape
    n = wk.shape[1]
    return pl.pallas_call(
        functools.partial(_proj_kv_kernel, tn=tn),
        grid=(b, s // t),
        in_specs=[pl.BlockSpec((1, t, k), lambda bi, i: (bi, i, 0)),
                  pl.BlockSpec((k, n), lambda bi, i: (0, 0)),
                  pl.BlockSpec((n, k), lambda bi, i: (0, 0))],
        out_specs=[pl.BlockSpec((1, t, n), lambda bi, i: (bi, i, 0)),
                   pl.BlockSpec((1, n, t), lambda bi, i: (bi, 0, i))],
        out_shape=[jax.ShapeDtypeStruct((b, s, n), BF16),
                   jax.ShapeDtypeStruct((b, n, s), BF16)],
        compiler_params=_params(("parallel", "parallel")),
        name="proj_kv",
    )(x, wk, wvt)


def _mla_kernel(q_ref, kn_ref, kr_ref, vt_ref, o_ref, kcat_ref, st_ref, m_ref, *,
                n_i, kc, n_steps):
    t = pl.program_id(0)
    s_len, tq = st_ref.shape
    sub8 = 8

    @pl.when(t == 0)
    def _():
        st_ref[...] = jnp.zeros(st_ref.shape, F32)
        m_ref[...] = jnp.zeros(m_ref.shape, F32)

    @pl.when((t % n_i == 0) & (t < n_steps))
    def _():
        kcat_ref[:, :LANES] = kn_ref[0]
        kcat_ref[:, LANES:] = kr_ref[0]

    q = q_ref[0]
    m_old = m_ref[...]
    m_new = None
    den8 = None
    ot = None
    for c in range(s_len // kc):
        rows = slice(c * kc, (c + 1) * kc)
        p = jnp.exp2(st_ref[rows, :] - m_old)
        part = jnp.sum(p.reshape(kc // sub8, sub8, tq), axis=0)
        den8 = part if den8 is None else den8 + part
        pv = jnp.dot(vt_ref[0, :, rows], p.astype(BF16), preferred_element_type=F32)
        ot = pv if ot is None else ot + pv
        s_new = lax.dot_general(kcat_ref[rows, :], q, (((1,), (1,)), ((), ())),
                                preferred_element_type=F32)
        st_ref[rows, :] = s_new
        part = jnp.max(s_new.reshape(kc // sub8, sub8, tq), axis=0)
        m_new = part if m_new is None else jnp.maximum(m_new, part)
    m_ref[...] = jnp.max(m_new, axis=0, keepdims=True)
    den = jnp.sum(den8, axis=0, keepdims=True)
    o_ref[0] = (ot / den).T.astype(BF16)


def _mla_attn(q, kn, kr, vt, *, tq=512, kc=512):
    b, s, _ = q.shape
    qw = 2 * LANES
    n_i = s // tq
    n_steps = b * B_HEADS * n_i

    def item(t, lag):
        t = jnp.clip(t - lag, 0, n_steps - 1)
        bh = t // n_i
        return bh // B_HEADS, bh % B_HEADS, t % n_i

    def q_map(t):
        bi, h, i = item(t, 0)
        return bi, i, h

    def kn_map(t):
        bi, h, _ = item(t, 0)
        return bi, 0, h

    def kr_map(t):
        bi, _, _ = item(t, 0)
        return bi, 0, 0

    def vt_map(t):
        bi, h, _ = item(t, 1)
        return bi, h, 0

    def o_map(t):
        bi, h, i = item(t, 1)
        return bi, i, h

    kern = functools.partial(_mla_kernel, n_i=n_i, kc=kc, n_steps=n_steps)
    return pl.pallas_call(
        kern,
        grid=(n_steps + 1,),
        in_specs=[
            pl.BlockSpec((1, tq, qw), q_map),
            pl.BlockSpec((1, s, NOPE_DIM), kn_map),
            pl.BlockSpec((1, s, LANES), kr_map),
            pl.BlockSpec((1, V_DIM, s), vt_map),
        ],
        out_specs=pl.BlockSpec((1, tq, V_DIM), o_map),
        out_shape=jax.ShapeDtypeStruct((b, s, B_HEADS * V_DIM), BF16),
        scratch_shapes=[pltpu.VMEM((s, qw), BF16),
                        pltpu.VMEM((s, tq), F32),
                        pltpu.VMEM((1, tq), F32)],
        compiler_params=_params(("arbitrary",)),
        name="mla_attn",
    )(q, kn, kr, vt)


def _out_proj_kernel(o_ref, w_ref, x_ref, y_ref, *, tn):
    for j in range(w_ref.shape[1] // tn):
        cols = slice(j * tn, (j + 1) * tn)
        y_ref[0, :, cols] = x_ref[0, :, cols] + jnp.dot(
            o_ref[0], w_ref[:, cols], preferred_element_type=F32)


def _out_proj(o, w, x, *, t=512, tn=512):
    b, s, k = o.shape
    n = w.shape[1]
    return pl.pallas_call(
        functools.partial(_out_proj_kernel, tn=tn),
        grid=(b, s // t),
        in_specs=[pl.BlockSpec((1, t, k), lambda bi, i: (bi, i, 0)),
                  pl.BlockSpec((k, n), lambda bi, i: (0, 0)),
                  pl.BlockSpec((1, t, n), lambda bi, i: (bi, i, 0))],
        out_specs=pl.BlockSpec((1, t, n), lambda bi, i: (bi, i, 0)),
        out_shape=jax.ShapeDtypeStruct((b, s, n), F32),
        compiler_params=_params(("parallel", "parallel")),
        name="out_proj",
    )(o, w, x)


def _ffn_kernel(xp_ref, x_ref, xx_ref, g_ref, *rest, t, n_tiles, nf, n_steps):
    w_refs, (y_ref, xn_ref, a0_ref, a1_ref, hg0_ref, hg1_ref, hu0_ref, hu1_ref) = (
        rest[:14], rest[14:])
    up_w = (w_refs[0:2], w_refs[2:4])
    conv_w = (w_refs[4:8], w_refs[8:12])
    down_w = w_refs[12:14]
    k = 2 * pl.program_id(0)
    f = k % nf
    i = (k // nf) % n_tiles
    halo = BF16_ROWS

    a_refs = (a0_ref, a1_ref)
    hg_refs = (hg0_ref, hg1_ref)
    hu_refs = (hu0_ref, hu1_ref)

    @pl.when(k == 0)
    def _():
        for ref in a_refs + hg_refs + hu_refs + (y_ref,):
            ref[...] = jnp.zeros(ref.shape, ref.dtype)

    @pl.when((f == 0) & (k < n_steps))
    def _():
        g = g_ref[...]
        prev = jnp.where(i > 0, _rms(xp_ref[0], g), 0.0)
        nxt = jnp.where(i < n_tiles - 1, _rms(xx_ref[0], g), 0.0)
        xn_ref[0:halo, :] = prev.astype(BF16)
        xn_ref[halo:halo + t, :] = _rms(x_ref[0], g).astype(BF16)
        xn_ref[halo + t:, :] = nxt.astype(BF16)

    @pl.when(f == 2)
    def _():
        y_ref[0] = x_ref[0]

    tf = a0_ref.shape[1]
    d = y_ref.shape[2]
    rb = 64
    blocks = [(r, c) for c in range(tf // LANES) for r in range(t // rb)]
    m_rows = t + 2 * halo
    up_rows = m_rows // (3 if m_rows % (3 * BF16_ROWS) == 0 else 2)
    down_rows = min(t, 512)
    down_cols = 512

    def conv(h_ref, r, cols, cw_ref, cb_ref):
        cw = cw_ref[:, cols]
        lo = halo + r * rb
        blk = h_ref[lo - 8:lo + rb + 8, cols]
        return (blk[7:7 + rb] * cw[0:1] + blk[8:8 + rb] * cw[1:2]
                + blk[9:9 + rb] * cw[2:3] + cb_ref[:, cols])

    def stages(p):
        wg_ref, wu_ref = up_w[p]
        cwg_ref, cwu_ref, cbg_ref, cbu_ref = conv_w[p]
        wo_ref = down_w[p]

        def gate_block(r, c):
            cols = slice(c * LANES, (c + 1) * LANES)
            gc = conv(hg_refs[1 - p], r, cols, cwg_ref, cbg_ref)
            uc = conv(hu_refs[1 - p], r, cols, cwu_ref, cbu_ref)
            a_refs[1 - p][r * rb:(r + 1) * rb, cols] = (
                gc * jax.nn.sigmoid(gc) * uc).astype(BF16)

        def up_piece(h_ref, w_ref, lo):
            h_ref[lo:lo + up_rows, :] = jnp.dot(xn_ref[lo:lo + up_rows, :], w_ref[0],
                                                preferred_element_type=F32)

        def down_piece(r0, c0):
            rows = slice(r0, r0 + down_rows)
            cols = slice(c0, c0 + down_cols)
            y_ref[0, rows, cols] += jnp.dot(a_refs[p][rows, :], wo_ref[:, cols],
                                            preferred_element_type=F32)

        pieces = [(functools.partial(up_piece, h_ref, w_ref, lo), up_rows * d)
                  for lo in range(0, m_rows, up_rows)
                  for h_ref, w_ref in ((hg_refs[p], wg_ref), (hu_refs[p], wu_ref))]
        pieces += [(functools.partial(down_piece, r0, c0), down_rows * down_cols)
                   for r0 in range(0, t, down_rows) for c0 in range(0, d, down_cols)]
        total = sum(w for _, w in pieces)
        done = 0
        acc = 0
        for piece, w in pieces:
            piece()
            acc += w
            upto = len(blocks) * acc // total
            for r, c in blocks[done:upto]:
                gate_block(r, c)
            done = upto

    stages(0)

    @pl.when(k >= 0)
    def _():
        stages(1)


FFN_TF = 256


def _ffn(x, g, w_in_tiles, conv_w, conv_b, w_out, *, t=1024):
    b, s, d = x.shape
    d_ff = w_out.shape[0]
    tf = w_in_tiles.shape[2]
    nf = d_ff // tf
    assert nf >= 4 and nf % 2 == 0 and w_in_tiles.shape[0] == 2 * nf
    n_tiles = s // t
    n_steps = b * n_tiles * nf
    halo = BF16_ROWS
    hb = t // halo
    last_hb = s // halo - 1

    def item(st, u, lag):
        k = jnp.clip(2 * st + u - lag, 0, n_steps - 1)
        tile = k // nf
        return tile // n_tiles, tile % n_tiles, k % nf

    def x_map(st):
        bi, i, _ = item(st, 0, 0)
        return bi, i, 0

    def xp_map(st):
        bi, i, _ = item(st, 0, 0)
        return bi, jnp.maximum(i * hb - 1, 0), 0

    def xx_map(st):
        bi, i, _ = item(st, 0, 0)
        return bi, jnp.minimum((i + 1) * hb, last_hb), 0

    def y_map(st):
        bi, i, _ = item(st, 0, 2)
        return bi, i, 0

    def w_in_spec(u, half):
        return pl.BlockSpec((1, d, tf), lambda st: (half * nf + item(st, u, 0)[2], 0, 0))

    def conv_spec(rows, u, half):
        return pl.BlockSpec((rows, tf), lambda st: (0, half * nf + item(st, u, 1)[2]))

    def w_out_spec(u):
        return pl.BlockSpec((tf, d), lambda st: (item(st, u, 2)[2], 0))

    w_specs = [w_in_spec(u, half) for u in (0, 1) for half in (0, 1)]
    w_args = [w_in_tiles] * 4
    for u in (0, 1):
        w_specs += [conv_spec(3, u, 0), conv_spec(3, u, 1),
                    conv_spec(1, u, 0), conv_spec(1, u, 1)]
        w_args += [conv_w, conv_w, conv_b, conv_b]
    w_specs += [w_out_spec(0), w_out_spec(1)]
    w_args += [w_out, w_out]

    kern = functools.partial(_ffn_kernel, t=t, n_tiles=n_tiles, nf=nf, n_steps=n_steps)
    return pl.pallas_call(
        kern,
        grid=(n_steps // 2 + 1,),
        in_specs=[
            pl.BlockSpec((1, halo, d), xp_map),
            pl.BlockSpec((1, t, d), x_map),
            pl.BlockSpec((1, halo, d), xx_map),
            pl.BlockSpec((1, d), lambda st: (0, 0)),
        ] + w_specs,
        out_specs=pl.BlockSpec((1, t, d), y_map),
        out_shape=jax.ShapeDtypeStruct((b, s, d), F32),
        scratch_shapes=[pltpu.VMEM((t + 2 * halo, d), BF16)]
        + [pltpu.VMEM((t, tf), BF16)] * 2
        + [pltpu.VMEM((t + 2 * halo, tf), F32)] * 4,
        compiler_params=_params(("arbitrary",)),
        name="ffn",
    )(x, x, x, g, *w_args)


def _final_norm_kernel(x_ref, g_ref, y_ref):
    y_ref[0] = _rms(x_ref[0], g_ref[...])


def _final_norm(x, g, *, t=1024):
    b, s, d = x.shape
    return pl.pallas_call(
        _final_norm_kernel,
        grid=(b, s // t),
        in_specs=[pl.BlockSpec((1, t, d), lambda bi, i: (bi, i, 0)),
                  pl.BlockSpec((1, d), lambda bi, i: (0, 0))],
        out_specs=pl.BlockSpec((1, t, d), lambda bi, i: (bi, i, 0)),
        out_shape=jax.ShapeDtypeStruct((b, s, d), F32),
        compiler_params=_params(("parallel", "parallel")),
        name="final_norm",
    )(x, g)


def _rope_tables(seq, dim):
    pos = jnp.arange(seq, dtype=F32)
    inv = 1.0 / (ROPE_THETA ** (jnp.arange(0, dim, 2, dtype=F32) / dim))
    ang = pos[:, None] * inv[None, :]
    return jnp.cos(ang), jnp.sin(ang)


def _prep_tables(seq):
    cos_a, sin_a = _rope_tables(seq, A_HEAD_DIM)
    cos2 = jnp.concatenate([cos_a, cos_a], axis=1)
    sin2 = jnp.concatenate([-sin_a, sin_a], axis=1)
    cos_b, sin_b = _rope_tables(seq, ROPE_DIM)
    half = ROPE_DIM // 2
    z = jnp.zeros((seq, half), F32)
    zz = jnp.zeros((seq, LANES - ROPE_DIM), F32)
    c = jnp.concatenate([cos_b, cos_b, zz], axis=1)
    s1 = jnp.concatenate([-sin_b, z, zz], axis=1)
    s2 = jnp.concatenate([z, sin_b, zz], axis=1)
    return (cos2, sin2), (c, s1, s2)


def _tile_cols_kernel(w_ref, o_ref):
    o_ref[0, 0] = w_ref[0].astype(BF16)


def _tile_cols(w, tn):
    layers, k, n = w.shape
    return pl.pallas_call(
        _tile_cols_kernel,
        grid=(layers, n // tn),
        in_specs=[pl.BlockSpec((1, k, tn), lambda l, j: (l, 0, j))],
        out_specs=pl.BlockSpec((1, 1, k, tn), lambda l, j: (l, j, 0, 0)),
        out_shape=jax.ShapeDtypeStruct((layers, n // tn, k, tn), BF16),
        compiler_params=_params(("parallel", "parallel")),
        name="tile_cols",
    )(w)


def _prep_b_weights(w_in, w_q_up, w_kv_up):
    pad = jnp.zeros((w_in.shape[0], LANES - ROPE_DIM), w_in.dtype)
    w_in_p = jnp.concatenate([w_in, pad], axis=1).astype(BF16)
    k = w_q_up.shape[0]
    wq = w_q_up.reshape(k, B_HEADS, NOPE_DIM + ROPE_DIM)
    wq = jnp.concatenate(
        [wq, jnp.zeros((k, B_HEADS, 2 * LANES - NOPE_DIM - ROPE_DIM), wq.dtype)], axis=2)
    wq = wq.reshape(k, B_HEADS * 2 * LANES).astype(BF16)
    wkv = w_kv_up.reshape(k, B_HEADS, NOPE_DIM + V_DIM)
    wk = wkv[:, :, :NOPE_DIM].reshape(k, B_HEADS * NOPE_DIM).astype(BF16)
    wvt = wkv[:, :, NOPE_DIM:].reshape(k, B_HEADS * V_DIM).T.astype(BF16)
    return w_in_p, wq, wk, wvt


def _trunk(x, p, tabs_a, tabs_b):
    depth = p["norm_mix"].shape[0]
    for i in range(depth):
        j = i // 2
        g_mix = p["norm_mix"][i][None, :]
        if i % 2 == 0:
            qkv = _qkv_a(x, g_mix, p["a_w_qkv"][j], *tabs_a)
            o = _win_attn(qkv, p["a_sink2"][j])
            x = _out_proj(o, p["a_w_o"][j], x)
        else:
            w_in_p, wq, wk, wvt = p["b_w"][j]
            cq, ckv, kr = _in_b(x, g_mix, w_in_p, p["b_q_norm"][j][None, :],
                                p["b_kv_norm"][j][None, :], *tabs_b)
            q = _proj_q(cq, wq, tabs_b)
            kn, vt = _proj_kv(ckv, wk, wvt)
            o = _mla_attn(q, kn, kr, vt)
            x = _out_proj(o, p["b_w_o"][j], x)
        x = _ffn(x, p["norm_ffn"][i][None, :], p["f_w_in"][i], p["f_conv_w"][i],
                 p["f_conv_b"][i][None, :], p["f_w_out"][i])
    return _final_norm(x, p["norm_final"][None, :])


def kernel(x_prompt, x_sample, norm_mix, norm_ffn, norm_final, a_w_qkv, a_w_o, a_sink,
           b_w_in, b_q_norm, b_w_q_up, b_kv_norm, b_w_kv_up, b_w_o,
           f_w_in, f_conv_w, f_conv_b, f_w_out):
    p = {
        "norm_mix": norm_mix, "norm_ffn": norm_ffn, "norm_final": norm_final,
        "a_w_qkv": a_w_qkv.astype(BF16), "a_w_o": a_w_o.astype(BF16),
        "a_sink2": a_sink * LOG2E,
        "b_q_norm": b_q_norm, "b_kv_norm": b_kv_norm, "b_w_o": b_w_o.astype(BF16),
        "b_w": [_prep_b_weights(b_w_in[j], b_w_q_up[j], b_w_kv_up[j])
                for j in range(b_w_in.shape[0])],
        "f_w_in": _tile_cols(f_w_in, FFN_TF), "f_conv_w": f_conv_w, "f_conv_b": f_conv_b,
        "f_w_out": f_w_out.astype(BF16),
    }
    outs = []
    for x in (x_prompt, x_sample):
        tabs_a, tabs_b = _prep_tables(x.shape[1])
        outs.append(_trunk(x, p, tabs_a, tabs_b))
    return tuple(outs)
```

```python
import functools
import math

import jax
import jax.numpy as jnp
from jax import lax
from jax.experimental import pallas as pl
from jax.experimental.pallas import tpu as pltpu

F32 = jnp.float32
BF16 = jnp.bfloat16

EPS = 1e-6
ROPE_THETA = 10000.0
LANES = 128
BF16_ROWS = 16
WINDOW = 128
A_HEADS = 16
A_KV_HEADS = 4
A_GROUP = A_HEADS // A_KV_HEADS
A_HEAD_DIM = 128
B_HEADS = 16
Q_LORA = 512
KV_LORA = 512
NOPE_DIM = 128
ROPE_DIM = 64
V_DIM = 128
LOG2E = math.log2(math.e)
NEG_BIG = -1e30
VMEM_LIMIT = 58 * 1024 * 1024


def _params(sem):
    return pltpu.CompilerParams(dimension_semantics=sem, vmem_limit_bytes=VMEM_LIMIT)


def _rms(x, g):
    return x * lax.rsqrt(jnp.mean(x * x, axis=-1, keepdims=True) + EPS) * g


def _rope64(x, c, s1, s2):
    return x * c + pltpu.roll(x, 96, 1) * s1 + pltpu.roll(x, 32, 1) * s2


def _qkv_a_kernel(x_ref, g_ref, w_ref, cos_ref, sin_ref, o_ref, xn_ref, *,
                  tn, q_cols, rope_cols, q_scale):
    xn_ref[...] = _rms(x_ref[0], g_ref[...]).astype(BF16)
    cos = cos_ref[...]
    sin = sin_ref[...]
    for j in range(w_ref.shape[1] // tn):
        acc = jnp.dot(xn_ref[...], w_ref[:, j * tn:(j + 1) * tn],
                      preferred_element_type=F32)
        for h in range(tn // LANES):
            col = j * tn + h * LANES
            ch = acc[:, h * LANES:(h + 1) * LANES]
            if col < rope_cols:
                ch = ch * cos + pltpu.roll(ch, LANES // 2, 1) * sin
            if col < q_cols:
                ch = ch * q_scale
            o_ref[0, :, col:col + LANES] = ch.astype(BF16)


def _qkv_a(x, g, w, cos2, sin2, *, t=512, tn=512):
    b, s, d = x.shape
    n = w.shape[1]
    q_cols = A_HEADS * A_HEAD_DIM
    k_cols = A_KV_HEADS * A_HEAD_DIM
    kern = functools.partial(
        _qkv_a_kernel, tn=tn, q_cols=q_cols, rope_cols=q_cols + k_cols,
        q_scale=A_HEAD_DIM ** -0.5 * LOG2E)
    return pl.pallas_call(
        kern,
        grid=(b, s // t),
        in_specs=[
            pl.BlockSpec((1, t, d), lambda bi, i: (bi, i, 0)),
            pl.BlockSpec((1, d), lambda bi, i: (0, 0)),
            pl.BlockSpec((d, n), lambda bi, i: (0, 0)),
            pl.BlockSpec((t, LANES), lambda bi, i: (i, 0)),
            pl.BlockSpec((t, LANES), lambda bi, i: (i, 0)),
        ],
        out_specs=pl.BlockSpec((1, t, n), lambda bi, i: (bi, i, 0)),
        out_shape=jax.ShapeDtypeStruct((b, s, n), BF16),
        scratch_shapes=[pltpu.VMEM((t, d), BF16)],
        compiler_params=_params(("parallel", "parallel")),
        name="qkv_a",
    )(x, g, w, cos2, sin2)


def _win_attn_kernel(sink_ref, q_ref, km_ref, kp_ref, kx_ref, vm_ref, vp_ref, vx_ref,
                     o_ref, kbuf, vbuf, *, tq, n_tiles):
    i = pl.program_id(1)
    blk = WINDOW
    span = 3 * blk
    kbuf[0:blk, :] = kp_ref[0]
    kbuf[blk:blk + tq, :] = km_ref[0]
    kbuf[blk + tq:, :] = kx_ref[0]
    vbuf[0:blk, :] = vp_ref[0]
    vbuf[blk:blk + tq, :] = vm_ref[0]
    vbuf[blk + tq:, :] = vx_ref[0]

    rows = A_GROUP * blk
    a_idx = lax.broadcasted_iota(jnp.int32, (rows, span), 0) % blk
    c_idx = lax.broadcasted_iota(jnp.int32, (rows, span), 1)
    rel = c_idx - a_idx
    band = (rel >= 0) & (rel <= 2 * WINDOW)
    grp = lax.broadcasted_iota(jnp.int32, (rows, 1), 0) // blk
    lo = jnp.where(i == 0, blk, 0)
    hi = jnp.where(i == n_tiles - 1, 2 * blk, span)

    n_sub = tq // blk
    for r in range(n_sub):
        valid = band
        if r == 0:
            valid = valid & (c_idx >= lo)
        if r == n_sub - 1:
            valid = valid & (c_idx < hi)
        for kh in range(A_KV_HEADS):
            qg = jnp.concatenate(
                [q_ref[0, r * blk:(r + 1) * blk,
                       (kh * A_GROUP + g) * A_HEAD_DIM:(kh * A_GROUP + g + 1) * A_HEAD_DIM]
                 for g in range(A_GROUP)], axis=0)
            kw = kbuf[r * blk:r * blk + span, kh * A_HEAD_DIM:(kh + 1) * A_HEAD_DIM]
            vw = vbuf[r * blk:r * blk + span, kh * A_HEAD_DIM:(kh + 1) * A_HEAD_DIM]
            s = lax.dot_general(qg, kw, (((1,), (1,)), ((), ())),
                                preferred_element_type=F32)
            s = jnp.where(valid, s, NEG_BIG)
            sk = jnp.zeros((rows, 1), F32)
            for g in range(A_GROUP):
                sk = jnp.where(grp == g, sink_ref[kh * A_GROUP + g], sk)
            m = jnp.maximum(jnp.max(s, axis=1, keepdims=True), sk)
            p = jnp.exp2(s - m)
            den = jnp.sum(p, axis=1, keepdims=True) + jnp.exp2(sk - m)
            o = jnp.dot(p.astype(BF16), vw, preferred_element_type=F32) / den
            for g in range(A_GROUP):
                col = (kh * A_GROUP + g) * A_HEAD_DIM
                o_ref[0, r * blk:(r + 1) * blk, col:col + A_HEAD_DIM] = (
                    o[g * blk:(g + 1) * blk].astype(BF16))


def _win_attn(qkv, sink2, *, tq=512):
    b, s, _ = qkv.shape
    d = A_HEADS * A_HEAD_DIM
    kd = A_KV_HEADS * A_HEAD_DIM
    blk = WINDOW
    n_tiles = s // tq
    sub = tq // blk
    nb = s // blk
    k_col = d // kd
    v_col = k_col + 1

    def main(col):
        return pl.BlockSpec((1, tq, kd), lambda bi, i: (bi, i, col))

    def prev(col):
        return pl.BlockSpec((1, blk, kd), lambda bi, i: (bi, jnp.maximum(i * sub - 1, 0), col))

    def nxt(col):
        return pl.BlockSpec((1, blk, kd), lambda bi, i: (bi, jnp.minimum((i + 1) * sub, nb - 1), col))

    kern = functools.partial(_win_attn_kernel, tq=tq, n_tiles=n_tiles)
    return pl.pallas_call(
        kern,
        grid=(b, n_tiles),
        in_specs=[
            pl.BlockSpec(memory_space=pltpu.SMEM),
            pl.BlockSpec((1, tq, d), lambda bi, i: (bi, i, 0)),
            main(k_col), prev(k_col), nxt(k_col),
            main(v_col), prev(v_col), nxt(v_col),
        ],
        out_specs=pl.BlockSpec((1, tq, d), lambda bi, i: (bi, i, 0)),
        out_shape=jax.ShapeDtypeStruct((b, s, d), BF16),
        scratch_shapes=[pltpu.VMEM((tq + 2 * blk, kd), BF16),
                        pltpu.VMEM((tq + 2 * blk, kd), BF16)],
        compiler_params=_params(("parallel", "parallel")),
        name="win_attn",
    )(sink2, qkv, qkv, qkv, qkv, qkv, qkv, qkv)


def _in_b_kernel(x_ref, g_ref, w_ref, qn_ref, kvn_ref, c_ref, s1_ref, s2_ref,
                 cq_ref, ckv_ref, kr_ref):
    xn = _rms(x_ref[0], g_ref[...]).astype(BF16)
    hq = jnp.dot(xn, w_ref[:, :Q_LORA], preferred_element_type=F32)
    hkv = jnp.dot(xn, w_ref[:, Q_LORA:Q_LORA + KV_LORA], preferred_element_type=F32)
    cq_ref[0] = _rms(hq, qn_ref[...]).astype(BF16)
    hr = jnp.dot(xn, w_ref[:, Q_LORA + KV_LORA:], preferred_element_type=F32)
    ckv_ref[0] = _rms(hkv, kvn_ref[...]).astype(BF16)
    kr_ref[0] = _rope64(hr, c_ref[...], s1_ref[...], s2_ref[...]).astype(BF16)


def _in_b(x, g, w, qn, kvn, c, s1, s2, *, t=512):
    b, s, d = x.shape
    n = w.shape[1]
    row = lambda bi, i: (bi, i, 0)
    fixed = lambda bi, i: (0, 0)
    tab = pl.BlockSpec((t, LANES), lambda bi, i: (i, 0))
    return pl.pallas_call(
        _in_b_kernel,
        grid=(b, s // t),
        in_specs=[
            pl.BlockSpec((1, t, d), row),
            pl.BlockSpec((1, d), fixed),
            pl.BlockSpec((d, n), fixed),
            pl.BlockSpec((1, Q_LORA), fixed),
            pl.BlockSpec((1, KV_LORA), fixed),
            tab, tab, tab,
        ],
        out_specs=[pl.BlockSpec((1, t, Q_LORA), row),
                   pl.BlockSpec((1, t, KV_LORA), row),
                   pl.BlockSpec((1, t, LANES), row)],
        out_shape=[jax.ShapeDtypeStruct((b, s, Q_LORA), BF16),
                   jax.ShapeDtypeStruct((b, s, KV_LORA), BF16),
                   jax.ShapeDtypeStruct((b, s, LANES), BF16)],
        compiler_params=_params(("parallel", "parallel")),
        name="in_b",
    )(x, g, w, qn, kvn, c, s1, s2)


def _proj_q_kernel(x_ref, w_ref, c_ref, s1_ref, s2_ref, o_ref, *, tn, scale):
    c = c_ref[...] * scale
    s1 = s1_ref[...] * scale
    s2 = s2_ref[...] * scale
    hw = 2 * LANES
    for j in range(w_ref.shape[1] // tn):
        acc = jnp.dot(x_ref[0], w_ref[:, j * tn:(j + 1) * tn], preferred_element_type=F32)
        for h in range(tn // hw):
            col = j * tn + h * hw
            o_ref[0, :, col:col + LANES] = (
                acc[:, h * hw:h * hw + LANES] * scale).astype(BF16)
            o_ref[0, :, col + LANES:col + hw] = _rope64(
                acc[:, h * hw + LANES:(h + 1) * hw], c, s1, s2).astype(BF16)


def _proj_q(x, w, tables, *, t=512, tn=1024):
    b, s, k = x.shape
    n = w.shape[1]
    kern = functools.partial(_proj_q_kernel, tn=tn,
                             scale=(NOPE_DIM + ROPE_DIM) ** -0.5 * LOG2E)
    tab = pl.BlockSpec((t, LANES), lambda bi, i: (i, 0))
    return pl.pallas_call(
        kern,
        grid=(b, s // t),
        in_specs=[pl.BlockSpec((1, t, k), lambda bi, i: (bi, i, 0)),
                  pl.BlockSpec((k, n), lambda bi, i: (0, 0)),
                  tab, tab, tab],
        out_specs=pl.BlockSpec((1, t, n), lambda bi, i: (bi, i, 0)),
        out_shape=jax.ShapeDtypeStruct((b, s, n), BF16),
        compiler_params=_params(("parallel", "parallel")),
        name="proj_q",
    )(x, w, *tables)


def _proj_kv_kernel(x_ref, wk_ref, wvt_ref, kn_ref, vt_ref, *, tn):
    x = x_ref[0]
    for j in range(wk_ref.shape[1] // tn):
        cols = slice(j * tn, (j + 1) * tn)
        kn_ref[0, :, cols] = jnp.dot(x, wk_ref[:, cols],
                                     preferred_element_type=F32).astype(BF16)
        vt_ref[0, cols, :] = lax.dot_general(wvt_ref[cols, :], x, (((1,), (1,)), ((), ())),
                                             preferred_element_type=F32).astype(BF16)


def _proj_kv(x, wk, wvt, *, t=512, tn=512):
    b, s, k = x.shape
    n = wk.shape[1]
    return pl.pallas_call(
        functools.partial(_proj_kv_kernel, tn=tn),
        grid=(b, s // t),
        in_specs=[pl.BlockSpec((1, t, k), lambda bi, i: (bi, i, 0)),
                  pl.BlockSpec((k, n), lambda bi, i: (0, 0)),
                  pl.BlockSpec((n, k), lambda bi, i: (0, 0))],
        out_specs=[pl.BlockSpec((1, t, n), lambda bi, i: (bi, i, 0)),
                   pl.BlockSpec((1, n, t), lambda bi, i: (bi, 0, i))],
        out_shape=[jax.ShapeDtypeStruct((b, s, n), BF16),
                   jax.ShapeDtypeStruct((b, n, s), BF16)],
        compiler_params=_params(("parallel", "parallel")),
        name="proj_kv",
    )(x, wk, wvt)


def _mla_kernel(q_ref, kn_ref, kr_ref, vt_ref, o_ref, kcat_ref, st_ref, m_ref, *,
                n_i, kc, n_steps):
    t = pl.program_id(0)
    s_len, tq = st_ref.shape
    sub8 = 8

    @pl.when(t == 0)
    def _():
        st_ref[...] = jnp.zeros(st_ref.shape, F32)
        m_ref[...] = jnp.zeros(m_ref.shape, F32)

    @pl.when((t % n_i == 0) & (t < n_steps))
    def _():
        kcat_ref[:, :LANES] = kn_ref[0]
        kcat_ref[:, LANES:] = kr_ref[0]

    q = q_ref[0]
    m_old = m_ref[...]
    m_new = None
    den8 = None
    ot = None
    for c in range(s_len // kc):
        rows = slice(c * kc, (c + 1) * kc)
        p = jnp.exp2(st_ref[rows, :] - m_old)
        part = jnp.sum(p.reshape(kc // sub8, sub8, tq), axis=0)
        den8 = part if den8 is None else den8 + part
        pv = jnp.dot(vt_ref[0, :, rows], p.astype(BF16), preferred_element_type=F32)
        ot = pv if ot is None else ot + pv
        s_new = lax.dot_general(kcat_ref[rows, :], q, (((1,), (1,)), ((), ())),
                                preferred_element_type=F32)
        st_ref[rows, :] = s_new
        part = jnp.max(s_new.reshape(kc // sub8, sub8, tq), axis=0)
        m_new = part if m_new is None else jnp.maximum(m_new, part)
    m_ref[...] = jnp.max(m_new, axis=0, keepdims=True)
    den = jnp.sum(den8, axis=0, keepdims=True)
    o_ref[0] = (ot / den).T.astype(BF16)


def _mla_attn(q, kn, kr, vt, *, tq=512, kc=512):
    b, s, _ = q.shape
    qw = 2 * LANES
    n_i = s // tq
    n_steps = b * B_HEADS * n_i

    def item(t, lag):
        t = jnp.clip(t - lag, 0, n_steps - 1)
        bh = t // n_i
        return bh // B_HEADS, bh % B_HEADS, t % n_i

    def q_map(t):
        bi, h, i = item(t, 0)
        return bi, i, h

    def kn_map(t):
        bi, h, _ = item(t, 0)
        return bi, 0, h

    def kr_map(t):
        bi, _, _ = item(t, 0)
        return bi, 0, 0

    def vt_map(t):
        bi, h, _ = item(t, 1)
        return bi, h, 0

    def o_map(t):
        bi, h, i = item(t, 1)
        return bi, i, h

    kern = functools.partial(_mla_kernel, n_i=n_i, kc=kc, n_steps=n_steps)
    return pl.pallas_call(
        kern,
        grid=(n_steps + 1,),
        in_specs=[
            pl.BlockSpec((1, tq, qw), q_map),
            pl.BlockSpec((1, s, NOPE_DIM), kn_map),
            pl.BlockSpec((1, s, LANES), kr_map),
            pl.BlockSpec((1, V_DIM, s), vt_map),
        ],
        out_specs=pl.BlockSpec((1, tq, V_DIM), o_map),
        out_shape=jax.ShapeDtypeStruct((b, s, B_HEADS * V_DIM), BF16),
        scratch_shapes=[pltpu.VMEM((s, qw), BF16),
                        pltpu.VMEM((s, tq), F32),
                        pltpu.VMEM((1, tq), F32)],
        compiler_params=_params(("arbitrary",)),
        name="mla_attn",
    )(q, kn, kr, vt)


def _out_proj_kernel(o_ref, w_ref, x_ref, y_ref, *, tn):
    for j in range(w_ref.shape[1] // tn):
        cols = slice(j * tn, (j + 1) * tn)
        y_ref[0, :, cols] = x_ref[0, :, cols] + jnp.dot(
            o_ref[0], w_ref[:, cols], preferred_element_type=F32)


def _out_proj(o, w, x, *, t=512, tn=512):
    b, s, k = o.shape
    n = w.shape[1]
    return pl.pallas_call(
        functools.partial(_out_proj_kernel, tn=tn),
        grid=(b, s // t),
        in_specs=[pl.BlockSpec((1, t, k), lambda bi, i: (bi, i, 0)),
                  pl.BlockSpec((k, n), lambda bi, i: (0, 0)),
                  pl.BlockSpec((1, t, n), lambda bi, i: (bi, i, 0))],
        out_specs=pl.BlockSpec((1, t, n), lambda bi, i: (bi, i, 0)),
        out_shape=jax.ShapeDtypeStruct((b, s, n), F32),
        compiler_params=_params(("parallel", "parallel")),
        name="out_proj",
    )(o, w, x)


def _ffn_kernel(xp_ref, x_ref, xx_ref, g_ref, *rest, t, n_tiles, nf, n_steps):
    w_refs, (y_ref, xn_ref, a0_ref, a1_ref, hg0_ref, hg1_ref, hu0_ref, hu1_ref) = (
        rest[:14], rest[14:])
    up_w = (w_refs[0:2], w_refs[2:4])
    conv_w = (w_refs[4:8], w_refs[8:12])
    down_w = w_refs[12:14]
    k = 2 * pl.program_id(0)
    f = k % nf
    i = (k // nf) % n_tiles
    halo = BF16_ROWS

    a_refs = (a0_ref, a1_ref)
    hg_refs = (hg0_ref, hg1_ref)
    hu_refs = (hu0_ref, hu1_ref)

    @pl.when(k == 0)
    def _():
        for ref in a_refs + hg_refs + hu_refs + (y_ref,):
            ref[...] = jnp.zeros(ref.shape, ref.dtype)

    @pl.when((f == 0) & (k < n_steps))
    def _():
        g = g_ref[...]
        prev = jnp.where(i > 0, _rms(xp_ref[0], g), 0.0)
        nxt = jnp.where(i < n_tiles - 1, _rms(xx_ref[0], g), 0.0)
        xn_ref[0:halo, :] = prev.astype(BF16)
        xn_ref[halo:halo + t, :] = _rms(x_ref[0], g).astype(BF16)
        xn_ref[halo + t:, :] = nxt.astype(BF16)

    @pl.when(f == 2)
    def _():
        y_ref[0] = x_ref[0]

    tf = a0_ref.shape[1]
    d = y_ref.shape[2]
    rb = 128
    blocks = [(r, c) for c in range(tf // LANES) for r in range(t // rb)]
    m_rows = t + 2 * halo
    up_rows = m_rows // (3 if m_rows % (3 * BF16_ROWS) == 0 else 2)
    down_rows = min(t, 512)
    down_cols = 512

    def conv(h_ref, r, cols, cw_ref, cb_ref):
        cw = cw_ref[:, cols]
        lo = halo + r * rb
        blk = h_ref[lo - 8:lo + rb + 8, cols]
        return (blk[7:7 + rb] * cw[0:1] + blk[8:8 + rb] * cw[1:2]
                + blk[9:9 + rb] * cw[2:3] + cb_ref[:, cols])

    def stages(p):
        wg_ref, wu_ref = up_w[p]
        cwg_ref, cwu_ref, cbg_ref, cbu_ref = conv_w[p]
        wo_ref = down_w[p]

        def gate_block(r, c):
            cols = slice(c * LANES, (c + 1) * LANES)
            gc = conv(hg_refs[1 - p], r, cols, cwg_ref, cbg_ref)
            uc = conv(hu_refs[1 - p], r, cols, cwu_ref, cbu_ref)
            a_refs[1 - p][r * rb:(r + 1) * rb, cols] = (
                gc * jax.nn.sigmoid(gc) * uc).astype(BF16)

        def up_piece(h_ref, w_ref, lo):
            h_ref[lo:lo + up_rows, :] = jnp.dot(xn_ref[lo:lo + up_rows, :], w_ref[0],
                                                preferred_element_type=F32)

        def down_piece(r0, c0):
            rows = slice(r0, r0 + down_rows)
            cols = slice(c0, c0 + down_cols)
            y_ref[0, rows, cols] += jnp.dot(a_refs[p][rows, :], wo_ref[:, cols],
                                            preferred_element_type=F32)

        pieces = [(functools.partial(up_piece, h_ref, w_ref, lo), up_rows * d)
                  for lo in range(0, m_rows, up_rows)
                  for h_ref, w_ref in ((hg_refs[p], wg_ref), (hu_refs[p], wu_ref))]
        pieces += [(functools.partial(down_piece, r0, c0), down_rows * down_cols)
                   for r0 in range(0, t, down_rows) for c0 in range(0, d, down_cols)]
        total = sum(w for _, w in pieces)
        done = 0
        acc = 0
        for piece, w in pieces:
            piece()
            acc += w
            upto = len(blocks) * acc // total
            for r, c in blocks[done:upto]:
                gate_block(r, c)
            done = upto

    stages(0)

    @pl.when(k >= 0)
    def _():
        stages(1)


FFN_TF = 256


def _ffn(x, g, w_in_tiles, conv_w, conv_b, w_out, *, t=1024):
    b, s, d = x.shape
    d_ff = w_out.shape[0]
    tf = w_in_tiles.shape[2]
    nf = d_ff // tf
    assert nf >= 4 and nf % 2 == 0 and w_in_tiles.shape[0] == 2 * nf
    n_tiles = s // t
    n_steps = b * n_tiles * nf
    halo = BF16_ROWS
    hb = t // halo
    last_hb = s // halo - 1

    def item(st, u, lag):
        k = jnp.clip(2 * st + u - lag, 0, n_steps - 1)
        tile = k // nf
        return tile // n_tiles, tile % n_tiles, k % nf

    def x_map(st):
        bi, i, _ = item(st, 0, 0)
        return bi, i, 0

    def xp_map(st):
        bi, i, _ = item(st, 0, 0)
        return bi, jnp.maximum(i * hb - 1, 0), 0

    def xx_map(st):
        bi, i, _ = item(st, 0, 0)
        return bi, jnp.minimum((i + 1) * hb, last_hb), 0

    def y_map(st):
        bi, i, _ = item(st, 0, 2)
        return bi, i, 0

    def w_in_spec(u, half):
        return pl.BlockSpec((1, d, tf), lambda st: (half * nf + item(st, u, 0)[2], 0, 0))

    def conv_spec(rows, u, half):
        return pl.BlockSpec((rows, tf), lambda st: (0, half * nf + item(st, u, 1)[2]))

    def w_out_spec(u):
        return pl.BlockSpec((tf, d), lambda st: (item(st, u, 2)[2], 0))

    w_specs = [w_in_spec(u, half) for u in (0, 1) for half in (0, 1)]
    w_args = [w_in_tiles] * 4
    for u in (0, 1):
        w_specs += [conv_spec(3, u, 0), conv_spec(3, u, 1),
                    conv_spec(1, u, 0), conv_spec(1, u, 1)]
        w_args += [conv_w, conv_w, conv_b, conv_b]
    w_specs += [w_out_spec(0), w_out_spec(1)]
    w_args += [w_out, w_out]

    kern = functools.partial(_ffn_kernel, t=t, n_tiles=n_tiles, nf=nf, n_steps=n_steps)
    return pl.pallas_call(
        kern,
        grid=(n_steps // 2 + 1,),
        in_specs=[
            pl.BlockSpec((1, halo, d), xp_map),
            pl.BlockSpec((1, t, d), x_map),
            pl.BlockSpec((1, halo, d), xx_map),
            pl.BlockSpec((1, d), lambda st: (0, 0)),
        ] + w_specs,
        out_specs=pl.BlockSpec((1, t, d), y_map),
        out_shape=jax.ShapeDtypeStruct((b, s, d), F32),
        scratch_shapes=[pltpu.VMEM((t + 2 * halo, d), BF16)]
        + [pltpu.VMEM((t, tf), BF16)] * 2
        + [pltpu.VMEM((t + 2 * halo, tf), F32)] * 4,
        compiler_params=_params(("arbitrary",)),
        name="ffn",
    )(x, x, x, g, *w_args)


def _final_norm_kernel(x_ref, g_ref, y_ref):
    y_ref[0] = _rms(x_ref[0], g_ref[...])


def _final_norm(x, g, *, t=1024):
    b, s, d = x.shape
    return pl.pallas_call(
        _final_norm_kernel,
        grid=(b, s // t),
        in_specs=[pl.BlockSpec((1, t, d), lambda bi, i: (bi, i, 0)),
                  pl.BlockSpec((1, d), lambda bi, i: (0, 0))],
        out_specs=pl.BlockSpec((1, t, d), lambda bi, i: (bi, i, 0)),
        out_shape=jax.ShapeDtypeStruct((b, s, d), F32),
        compiler_params=_params(("parallel", "parallel")),
        name="final_norm",
    )(x, g)


def _rope_tables(seq, dim):
    pos = jnp.arange(seq, dtype=F32)
    inv = 1.0 / (ROPE_THETA ** (jnp.arange(0, dim, 2, dtype=F32) / dim))
    ang = pos[:, None] * inv[None, :]
    return jnp.cos(ang), jnp.sin(ang)


def _prep_tables(seq):
    cos_a, sin_a = _rope_tables(seq, A_HEAD_DIM)
    cos2 = jnp.concatenate([cos_a, cos_a], axis=1)
    sin2 = jnp.concatenate([-sin_a, sin_a], axis=1)
    cos_b, sin_b = _rope_tables(seq, ROPE_DIM)
    half = ROPE_DIM // 2
    z = jnp.zeros((seq, half), F32)
    zz = jnp.zeros((seq, LANES - ROPE_DIM), F32)
    c = jnp.concatenate([cos_b, cos_b, zz], axis=1)
    s1 = jnp.concatenate([-sin_b, z, zz], axis=1)
    s2 = jnp.concatenate([z, sin_b, zz], axis=1)
    return (cos2, sin2), (c, s1, s2)


def _tile_cols_kernel(w_ref, o_ref):
    o_ref[0, 0] = w_ref[0].astype(BF16)


def _tile_cols(w, tn):
    layers, k, n = w.shape
    return pl.pallas_call(
        _tile_cols_kernel,
        grid=(layers, n // tn),
        in_specs=[pl.BlockSpec((1, k, tn), lambda l, j: (l, 0, j))],
        out_specs=pl.BlockSpec((1, 1, k, tn), lambda l, j: (l, j, 0, 0)),
        out_shape=jax.ShapeDtypeStruct((layers, n // tn, k, tn), BF16),
        compiler_params=_params(("parallel", "parallel")),
        name="tile_cols",
    )(w)


def _prep_b_weights(w_in, w_q_up, w_kv_up):
    pad = jnp.zeros((w_in.shape[0], LANES - ROPE_DIM), w_in.dtype)
    w_in_p = jnp.concatenate([w_in, pad], axis=1).astype(BF16)
    k = w_q_up.shape[0]
    wq = w_q_up.reshape(k, B_HEADS, NOPE_DIM + ROPE_DIM)
    wq = jnp.concatenate(
        [wq, jnp.zeros((k, B_HEADS, 2 * LANES - NOPE_DIM - ROPE_DIM), wq.dtype)], axis=2)
    wq = wq.reshape(k, B_HEADS * 2 * LANES).astype(BF16)
    wkv = w_kv_up.reshape(k, B_HEADS, NOPE_DIM + V_DIM)
    wk = wkv[:, :, :NOPE_DIM].reshape(k, B_HEADS * NOPE_DIM).astype(BF16)
    wvt = wkv[:, :, NOPE_DIM:].reshape(k, B_HEADS * V_DIM).T.astype(BF16)
    return w_in_p, wq, wk, wvt


def _trunk(x, p, tabs_a, tabs_b):
    depth = p["norm_mix"].shape[0]
    for i in range(depth):
        j = i // 2
        g_mix = p["norm_mix"][i][None, :]
        if i % 2 == 0:
            qkv = _qkv_a(x, g_mix, p["a_w_qkv"][j], *tabs_a)
            o = _win_attn(qkv, p["a_sink2"][j])
            x = _out_proj(o, p["a_w_o"][j], x)
        else:
            w_in_p, wq, wk, wvt = p["b_w"][j]
            cq, ckv, kr = _in_b(x, g_mix, w_in_p, p["b_q_norm"][j][None, :],
                                p["b_kv_norm"][j][None, :], *tabs_b)
            q = _proj_q(cq, wq, tabs_b)
            kn, vt = _proj_kv(ckv, wk, wvt)
            o = _mla_attn(q, kn, kr, vt)
            x = _out_proj(o, p["b_w_o"][j], x)
        x = _ffn(x, p["norm_ffn"][i][None, :], p["f_w_in"][i], p["f_conv_w"][i],
                 p["f_conv_b"][i][None, :], p["f_w_out"][i])
    return _final_norm(x, p["norm_final"][None, :])


def kernel(x_prompt, x_sample, norm_mix, norm_ffn, norm_final, a_w_qkv, a_w_o, a_sink,
           b_w_in, b_q_norm, b_w_q_up, b_kv_norm, b_w_kv_up, b_w_o,
           f_w_in, f_conv_w, f_conv_b, f_w_out):
    p = {
        "norm_mix": norm_mix, "norm_ffn": norm_ffn, "norm_final": norm_final,
        "a_w_qkv": a_w_qkv.astype(BF16), "a_w_o": a_w_o.astype(BF16),
        "a_sink2": a_sink * LOG2E,
        "b_q_norm": b_q_norm, "b_kv_norm": b_kv_norm, "b_w_o": b_w_o.astype(BF16),
        "b_w": [_prep_b_weights(b_w_in[j], b_w_q_up[j], b_w_kv_up[j])
                for j in range(b_w_in.shape[0])],
        "f_w_in": _tile_cols(f_w_in, FFN_TF), "f_conv_w": f_conv_w, "f_conv_b": f_conv_b,
        "f_w_out": f_w_out.astype(BF16),
    }
    outs = []
    for x in (x_prompt, x_sample):
        tabs_a, tabs_b = _prep_tables(x.shape[1])
        outs.append(_trunk(x, p, tabs_a, tabs_b))
    return tuple(outs)
```

```python
import functools
import math

import jax
import jax.numpy as jnp
from jax import lax
from jax.experimental import pallas as pl
from jax.experimental.pallas import tpu as pltpu

F32 = jnp.float32
BF16 = jnp.bfloat16

EPS = 1e-6
ROPE_THETA = 10000.0
LANES = 128
BF16_ROWS = 16
WINDOW = 128
A_HEADS = 16
A_KV_HEADS = 4
A_GROUP = A_HEADS // A_KV_HEADS
A_HEAD_DIM = 128
B_HEADS = 16
Q_LORA = 512
KV_LORA = 512
NOPE_DIM = 128
ROPE_DIM = 64
V_DIM = 128
LOG2E = math.log2(math.e)
NEG_BIG = -1e30
VMEM_LIMIT = 58 * 1024 * 1024


def _params(sem):
    return pltpu.CompilerParams(dimension_semantics=sem, vmem_limit_bytes=VMEM_LIMIT)


def _rms(x, g):
    return x * lax.rsqrt(jnp.mean(x * x, axis=-1, keepdims=True) + EPS) * g


def _rope64(x, c, s1, s2):
    return x * c + pltpu.roll(x, 96, 1) * s1 + pltpu.roll(x, 32, 1) * s2


def _qkv_a_kernel(x_ref, g_ref, w_ref, cos_ref, sin_ref, o_ref, xn_ref, *,
                  tn, q_cols, rope_cols, q_scale):
    xn_ref[...] = _rms(x_ref[0], g_ref[...]).astype(BF16)
    cos = cos_ref[...]
    sin = sin_ref[...]
    for j in range(w_ref.shape[1] // tn):
        acc = jnp.dot(xn_ref[...], w_ref[:, j * tn:(j + 1) * tn],
                      preferred_element_type=F32)
        for h in range(tn // LANES):
            col = j * tn + h * LANES
            ch = acc[:, h * LANES:(h + 1) * LANES]
            if col < rope_cols:
                ch = ch * cos + pltpu.roll(ch, LANES // 2, 1) * sin
            if col < q_cols:
                ch = ch * q_scale
            o_ref[0, :, col:col + LANES] = ch.astype(BF16)


def _qkv_a(x, g, w, cos2, sin2, *, t=512, tn=512):
    b, s, d = x.shape
    n = w.shape[1]
    q_cols = A_HEADS * A_HEAD_DIM
    k_cols = A_KV_HEADS * A_HEAD_DIM
    kern = functools.partial(
        _qkv_a_kernel, tn=tn, q_cols=q_cols, rope_cols=q_cols + k_cols,
        q_scale=A_HEAD_DIM ** -0.5 * LOG2E)
    return pl.pallas_call(
        kern,
        grid=(b, s // t),
        in_specs=[
            pl.BlockSpec((1, t, d), lambda bi, i: (bi, i, 0)),
            pl.BlockSpec((1, d), lambda bi, i: (0, 0)),
            pl.BlockSpec((d, n), lambda bi, i: (0, 0)),
            pl.BlockSpec((t, LANES), lambda bi, i: (i, 0)),
            pl.BlockSpec((t, LANES), lambda bi, i: (i, 0)),
        ],
        out_specs=pl.BlockSpec((1, t, n), lambda bi, i: (bi, i, 0)),
        out_shape=jax.ShapeDtypeStruct((b, s, n), BF16),
        scratch_shapes=[pltpu.VMEM((t, d), BF16)],
        compiler_params=_params(("parallel", "parallel")),
        name="qkv_a",
    )(x, g, w, cos2, sin2)


def _win_attn_kernel(sink_ref, q_ref, km_ref, kp_ref, kx_ref, vm_ref, vp_ref, vx_ref,
                     o_ref, kbuf, vbuf, *, tq, n_tiles):
    i = pl.program_id(1)
    blk = WINDOW
    span = 3 * blk
    kbuf[0:blk, :] = kp_ref[0]
    kbuf[blk:blk + tq, :] = km_ref[0]
    kbuf[blk + tq:, :] = kx_ref[0]
    vbuf[0:blk, :] = vp_ref[0]
    vbuf[blk:blk + tq, :] = vm_ref[0]
    vbuf[blk + tq:, :] = vx_ref[0]

    rows = A_GROUP * blk
    a_idx = lax.broadcasted_iota(jnp.int32, (rows, span), 0) % blk
    c_idx = lax.broadcasted_iota(jnp.int32, (rows, span), 1)
    rel = c_idx - a_idx
    band = (rel >= 0) & (rel <= 2 * WINDOW)
    grp = lax.broadcasted_iota(jnp.int32, (rows, 1), 0) // blk
    lo = jnp.where(i == 0, blk, 0)
    hi = jnp.where(i == n_tiles - 1, 2 * blk, span)

    n_sub = tq // blk
    for r in range(n_sub):
        valid = band
        if r == 0:
            valid = valid & (c_idx >= lo)
        if r == n_sub - 1:
            valid = valid & (c_idx < hi)
        for kh in range(A_KV_HEADS):
            qg = jnp.concatenate(
                [q_ref[0, r * blk:(r + 1) * blk,
                       (kh * A_GROUP + g) * A_HEAD_DIM:(kh * A_GROUP + g + 1) * A_HEAD_DIM]
                 for g in range(A_GROUP)], axis=0)
            kw = kbuf[r * blk:r * blk + span, kh * A_HEAD_DIM:(kh + 1) * A_HEAD_DIM]
            vw = vbuf[r * blk:r * blk + span, kh * A_HEAD_DIM:(kh + 1) * A_HEAD_DIM]
            s = lax.dot_general(qg, kw, (((1,), (1,)), ((), ())),
                                preferred_element_type=F32)
            s = jnp.where(valid, s, NEG_BIG)
            sk = jnp.zeros((rows, 1), F32)
            for g in range(A_GROUP):
                sk = jnp.where(grp == g, sink_ref[kh * A_GROUP + g], sk)
            m = jnp.maximum(jnp.max(s, axis=1, keepdims=True), sk)
            p = jnp.exp2(s - m)
            den = jnp.sum(p, axis=1, keepdims=True) + jnp.exp2(sk - m)
            o = jnp.dot(p.astype(BF16), vw, preferred_element_type=F32) / den
            for g in range(A_GROUP):
                col = (kh * A_GROUP + g) * A_HEAD_DIM
                o_ref[0, r * blk:(r + 1) * blk, col:col + A_HEAD_DIM] = (
                    o[g * blk:(g + 1) * blk].astype(BF16))


def _win_attn(qkv, sink2, *, tq=1024):
    b, s, _ = qkv.shape
    d = A_HEADS * A_HEAD_DIM
    kd = A_KV_HEADS * A_HEAD_DIM
    blk = WINDOW
    n_tiles = s // tq
    sub = tq // blk
    nb = s // blk
    k_col = d // kd
    v_col = k_col + 1

    def main(col):
        return pl.BlockSpec((1, tq, kd), lambda bi, i: (bi, i, col))

    def prev(col):
        return pl.BlockSpec((1, blk, kd), lambda bi, i: (bi, jnp.maximum(i * sub - 1, 0), col))

    def nxt(col):
        return pl.BlockSpec((1, blk, kd), lambda bi, i: (bi, jnp.minimum((i + 1) * sub, nb - 1), col))

    kern = functools.partial(_win_attn_kernel, tq=tq, n_tiles=n_tiles)
    return pl.pallas_call(
        kern,
        grid=(b, n_tiles),
        in_specs=[
            pl.BlockSpec(memory_space=pltpu.SMEM),
            pl.BlockSpec((1, tq, d), lambda bi, i: (bi, i, 0)),
            main(k_col), prev(k_col), nxt(k_col),
            main(v_col), prev(v_col), nxt(v_col),
        ],
        out_specs=pl.BlockSpec((1, tq, d), lambda bi, i: (bi, i, 0)),
        out_shape=jax.ShapeDtypeStruct((b, s, d), BF16),
        scratch_shapes=[pltpu.VMEM((tq + 2 * blk, kd), BF16),
                        pltpu.VMEM((tq + 2 * blk, kd), BF16)],
        compiler_params=_params(("parallel", "parallel")),
        name="win_attn",
    )(sink2, qkv, qkv, qkv, qkv, qkv, qkv, qkv)


def _in_b_kernel(x_ref, g_ref, w_ref, qn_ref, kvn_ref, c_ref, s1_ref, s2_ref,
                 cq_ref, ckv_ref, kr_ref):
    xn = _rms(x_ref[0], g_ref[...]).astype(BF16)
    hq = jnp.dot(xn, w_ref[:, :Q_LORA], preferred_element_type=F32)
    hkv = jnp.dot(xn, w_ref[:, Q_LORA:Q_LORA + KV_LORA], preferred_element_type=F32)
    cq_ref[0] = _rms(hq, qn_ref[...]).astype(BF16)
    hr = jnp.dot(xn, w_ref[:, Q_LORA + KV_LORA:], preferred_element_type=F32)
    ckv_ref[0] = _rms(hkv, kvn_ref[...]).astype(BF16)
    kr_ref[0] = _rope64(hr, c_ref[...], s1_ref[...], s2_ref[...]).astype(BF16)


def _in_b(x, g, w, qn, kvn, c, s1, s2, *, t=1024):
    b, s, d = x.shape
    n = w.shape[1]
    row = lambda bi, i: (bi, i, 0)
    fixed = lambda bi, i: (0, 0)
    tab = pl.BlockSpec((t, LANES), lambda bi, i: (i, 0))
    return pl.pallas_call(
        _in_b_kernel,
        grid=(b, s // t),
        in_specs=[
            pl.BlockSpec((1, t, d), row),
            pl.BlockSpec((1, d), fixed),
            pl.BlockSpec((d, n), fixed),
            pl.BlockSpec((1, Q_LORA), fixed),
            pl.BlockSpec((1, KV_LORA), fixed),
            tab, tab, tab,
        ],
        out_specs=[pl.BlockSpec((1, t, Q_LORA), row),
                   pl.BlockSpec((1, t, KV_LORA), row),
                   pl.BlockSpec((1, t, LANES), row)],
        out_shape=[jax.ShapeDtypeStruct((b, s, Q_LORA), BF16),
                   jax.ShapeDtypeStruct((b, s, KV_LORA), BF16),
                   jax.ShapeDtypeStruct((b, s, LANES), BF16)],
        compiler_params=_params(("parallel", "parallel")),
        name="in_b",
    )(x, g, w, qn, kvn, c, s1, s2)


def _proj_q_kernel(x_ref, w_ref, c_ref, s1_ref, s2_ref, o_ref, *, tn, scale):
    c = c_ref[...] * scale
    s1 = s1_ref[...] * scale
    s2 = s2_ref[...] * scale
    hw = 2 * LANES
    for j in range(w_ref.shape[1] // tn):
        acc = jnp.dot(x_ref[0], w_ref[:, j * tn:(j + 1) * tn], preferred_element_type=F32)
        for h in range(tn // hw):
            col = j * tn + h * hw
            o_ref[0, :, col:col + LANES] = (
                acc[:, h * hw:h * hw + LANES] * scale).astype(BF16)
            o_ref[0, :, col + LANES:col + hw] = _rope64(
                acc[:, h * hw + LANES:(h + 1) * hw], c, s1, s2).astype(BF16)


def _proj_q(x, w, tables, *, t=1024, tn=1024):
    b, s, k = x.shape
    n = w.shape[1]
    kern = functools.partial(_proj_q_kernel, tn=tn,
                             scale=(NOPE_DIM + ROPE_DIM) ** -0.5 * LOG2E)
    tab = pl.BlockSpec((t, LANES), lambda bi, i: (i, 0))
    return pl.pallas_call(
        kern,
        grid=(b, s // t),
        in_specs=[pl.BlockSpec((1, t, k), lambda bi, i: (bi, i, 0)),
                  pl.BlockSpec((k, n), lambda bi, i: (0, 0)),
                  tab, tab, tab],
        out_specs=pl.BlockSpec((1, t, n), lambda bi, i: (bi, i, 0)),
        out_shape=jax.ShapeDtypeStruct((b, s, n), BF16),
        compiler_params=_params(("parallel", "parallel")),
        name="proj_q",
    )(x, w, *tables)


def _proj_kv_kernel(x_ref, wk_ref, wvt_ref, kn_ref, vt_ref, *, tn):
    x = x_ref[0]
    for j in range(wk_ref.shape[1] // tn):
        cols = slice(j * tn, (j + 1) * tn)
        kn_ref[0, :, cols] = jnp.dot(x, wk_ref[:, cols],
                                     preferred_element_type=F32).astype(BF16)
        vt_ref[0, cols, :] = lax.dot_general(wvt_ref[cols, :], x, (((1,), (1,)), ((), ())),
                                             preferred_element_type=F32).astype(BF16)


def _proj_kv(x, wk, wvt, *, t=1024, tn=512):
    b, s, k = x.shape
    n = wk.shape[1]
    return pl.pallas_call(
        functools.partial(_proj_kv_kernel, tn=tn),
        grid=(b, s // t),
        in_specs=[pl.BlockSpec((1, t, k), lambda bi, i: (bi, i, 0)),
                  pl.BlockSpec((k, n), lambda bi, i: (0, 0)),
                  pl.BlockSpec((n, k), lambda bi, i: (0, 0))],
        out_specs=[pl.BlockSpec((1, t, n), lambda bi, i: (bi, i, 0)),
                   pl.BlockSpec((1, n, t), lambda bi, i: (bi, 0, i))],
        out_shape=[jax.ShapeDtypeStruct((b, s, n), BF16),
                   jax.ShapeDtypeStruct((b, n, s), BF16)],
        compiler_params=_params(("parallel", "parallel")),
        name="proj_kv",
    )(x, wk, wvt)


def _mla_kernel(q_ref, kn_ref, kr_ref, vt_ref, o_ref, kcat_ref, st_ref, m_ref, *,
                n_i, kc, n_steps):
    t = pl.program_id(0)
    s_len, tq = st_ref.shape
    sub8 = 8

    @pl.when(t == 0)
    def _():
        st_ref[...] = jnp.zeros(st_ref.shape, F32)
        m_ref[...] = jnp.zeros(m_ref.shape, F32)

    @pl.when((t % n_i == 0) & (t < n_steps))
    def _():
        kcat_ref[:, :LANES] = kn_ref[0]
        kcat_ref[:, LANES:] = kr_ref[0]

    q = q_ref[0]
    m_old = m_ref[...]
    m_new = None
    den8 = None
    ot = None
    for c in range(s_len // kc):
        rows = slice(c * kc, (c + 1) * kc)
        p = jnp.exp2(st_ref[rows, :] - m_old)
        part = jnp.sum(p.reshape(kc // sub8, sub8, tq), axis=0)
        den8 = part if den8 is None else den8 + part
        pv = jnp.dot(vt_ref[0, :, rows], p.astype(BF16), preferred_element_type=F32)
        ot = pv if ot is None else ot + pv
        s_new = lax.dot_general(kcat_ref[rows, :], q, (((1,), (1,)), ((), ())),
                                preferred_element_type=F32)
        st_ref[rows, :] = s_new
        part = jnp.max(s_new.reshape(kc // sub8, sub8, tq), axis=0)
        m_new = part if m_new is None else jnp.maximum(m_new, part)
    m_ref[...] = jnp.max(m_new, axis=0, keepdims=True)
    den = jnp.sum(den8, axis=0, keepdims=True)
    o_ref[0] = (ot / den).T.astype(BF16)


def _mla_attn(q, kn, kr, vt, *, tq=512, kc=512):
    b, s, _ = q.shape
    qw = 2 * LANES
    n_i = s // tq
    n_steps = b * B_HEADS * n_i

    def item(t, lag):
        t = jnp.clip(t - lag, 0, n_steps - 1)
        bh = t // n_i
        return bh // B_HEADS, bh % B_HEADS, t % n_i

    def q_map(t):
        bi, h, i = item(t, 0)
        return bi, i, h

    def kn_map(t):
        bi, h, _ = item(t, 0)
        return bi, 0, h

    def kr_map(t):
        bi, _, _ = item(t, 0)
        return bi, 0, 0

    def vt_map(t):
        bi, h, _ = item(t, 1)
        return bi, h, 0

    def o_map(t):
        bi, h, i = item(t, 1)
        return bi, i, h

    kern = functools.partial(_mla_kernel, n_i=n_i, kc=kc, n_steps=n_steps)
    return pl.pallas_call(
        kern,
        grid=(n_steps + 1,),
        in_specs=[
            pl.BlockSpec((1, tq, qw), q_map),
            pl.BlockSpec((1, s, NOPE_DIM), kn_map),
            pl.BlockSpec((1, s, LANES), kr_map),
            pl.BlockSpec((1, V_DIM, s), vt_map),
        ],
        out_specs=pl.BlockSpec((1, tq, V_DIM), o_map),
        out_shape=jax.ShapeDtypeStruct((b, s, B_HEADS * V_DIM), BF16),
        scratch_shapes=[pltpu.VMEM((s, qw), BF16),
                        pltpu.VMEM((s, tq), F32),
                        pltpu.VMEM((1, tq), F32)],
        compiler_params=_params(("arbitrary",)),
        name="mla_attn",
    )(q, kn, kr, vt)


def _out_proj_kernel(o_ref, w_ref, x_ref, y_ref, *, tn):
    for j in range(w_ref.shape[1] // tn):
        cols = slice(j * tn, (j + 1) * tn)
        y_ref[0, :, cols] = x_ref[0, :, cols] + jnp.dot(
            o_ref[0], w_ref[:, cols], preferred_element_type=F32)


def _out_proj(o, w, x, *, t=512, tn=512):
    b, s, k = o.shape
    n = w.shape[1]
    return pl.pallas_call(
        functools.partial(_out_proj_kernel, tn=tn),
        grid=(b, s // t),
        in_specs=[pl.BlockSpec((1, t, k), lambda bi, i: (bi, i, 0)),
                  pl.BlockSpec((k, n), lambda bi, i: (0, 0)),
                  pl.BlockSpec((1, t, n), lambda bi, i: (bi, i, 0))],
        out_specs=pl.BlockSpec((1, t, n), lambda bi, i: (bi, i, 0)),
        out_shape=jax.ShapeDtypeStruct((b, s, n), F32),
        compiler_params=_params(("parallel", "parallel")),
        name="out_proj",
    )(o, w, x)


def _ffn_kernel(xp_ref, x_ref, xx_ref, g_ref, *rest, t, n_tiles, nf, n_steps):
    w_refs, (y_ref, xn_ref, a0_ref, a1_ref, hg0_ref, hg1_ref, hu0_ref, hu1_ref) = (
        rest[:14], rest[14:])
    up_w = (w_refs[0:2], w_refs[2:4])
    conv_w = (w_refs[4:8], w_refs[8:12])
    down_w = w_refs[12:14]
    k = 2 * pl.program_id(0)
    f = k % nf
    i = (k // nf) % n_tiles
    halo = BF16_ROWS

    a_refs = (a0_ref, a1_ref)
    hg_refs = (hg0_ref, hg1_ref)
    hu_refs = (hu0_ref, hu1_ref)

    @pl.when(k == 0)
    def _():
        for ref in a_refs + hg_refs + hu_refs + (y_ref,):
            ref[...] = jnp.zeros(ref.shape, ref.dtype)

    @pl.when((f == 0) & (k < n_steps))
    def _():
        g = g_ref[...]
        prev = jnp.where(i > 0, _rms(xp_ref[0], g), 0.0)
        nxt = jnp.where(i < n_tiles - 1, _rms(xx_ref[0], g), 0.0)
        xn_ref[0:halo, :] = prev.astype(BF16)
        xn_ref[halo:halo + t, :] = _rms(x_ref[0], g).astype(BF16)
        xn_ref[halo + t:, :] = nxt.astype(BF16)

    @pl.when(f == 2)
    def _():
        y_ref[0] = x_ref[0]

    tf = a0_ref.shape[1]
    d = y_ref.shape[2]
    rb = 128
    blocks = [(r, c) for c in range(tf // LANES) for r in range(t // rb)]
    m_rows = t + 2 * halo
    up_rows = m_rows // (3 if m_rows % (3 * BF16_ROWS) == 0 else 2)
    down_rows = min(t, 512)
    down_cols = 512

    def conv(h_ref, r, cols, cw_ref, cb_ref):
        cw = cw_ref[:, cols]
        lo = halo + r * rb
        blk = h_ref[lo - 8:lo + rb + 8, cols]
        return (blk[7:7 + rb] * cw[0:1] + blk[8:8 + rb] * cw[1:2]
                + blk[9:9 + rb] * cw[2:3] + cb_ref[:, cols])

    def stages(p):
        wg_ref, wu_ref = up_w[p]
        cwg_ref, cwu_ref, cbg_ref, cbu_ref = conv_w[p]
        wo_ref = down_w[p]

        def gate_block(r, c):
            cols = slice(c * LANES, (c + 1) * LANES)
            gc = conv(hg_refs[1 - p], r, cols, cwg_ref, cbg_ref)
            uc = conv(hu_refs[1 - p], r, cols, cwu_ref, cbu_ref)
            a_refs[1 - p][r * rb:(r + 1) * rb, cols] = (
                gc * jax.nn.sigmoid(gc) * uc).astype(BF16)

        def up_piece(h_ref, w_ref, lo):
            h_ref[lo:lo + up_rows, :] = jnp.dot(xn_ref[lo:lo + up_rows, :], w_ref[0],
                                                preferred_element_type=F32)

        def down_piece(r0, c0):
            rows = slice(r0, r0 + down_rows)
            cols = slice(c0, c0 + down_cols)
            y_ref[0, rows, cols] += jnp.dot(a_refs[p][rows, :], wo_ref[:, cols],
                                            preferred_element_type=F32)

        pieces = [(functools.partial(up_piece, h_ref, w_ref, lo), up_rows * d)
                  for lo in range(0, m_rows, up_rows)
                  for h_ref, w_ref in ((hg_refs[p], wg_ref), (hu_refs[p], wu_ref))]
        pieces += [(functools.partial(down_piece, r0, c0), down_rows * down_cols)
                   for r0 in range(0, t, down_rows) for c0 in range(0, d, down_cols)]
        total = sum(w for _, w in pieces)
        done = 0
        acc = 0
        for piece, w in pieces:
            piece()
            acc += w
            upto = len(blocks) * acc // total
            for r, c in blocks[done:upto]:
                gate_block(r, c)
            done = upto

    stages(0)

    @pl.when(k >= 0)
    def _():
        stages(1)


FFN_TF = 256


def _ffn(x, g, w_in_tiles, conv_w, conv_b, w_out, *, t=1024):
    b, s, d = x.shape
    d_ff = w_out.shape[0]
    tf = w_in_tiles.shape[2]
    nf = d_ff // tf
    assert nf >= 4 and nf % 2 == 0 and w_in_tiles.shape[0] == 2 * nf
    n_tiles = s // t
    n_steps = b * n_tiles * nf
    halo = BF16_ROWS
    hb = t // halo
    last_hb = s // halo - 1

    def item(st, u, lag):
        k = jnp.clip(2 * st + u - lag, 0, n_steps - 1)
        tile = k // nf
        return tile // n_tiles, tile % n_tiles, k % nf

    def x_map(st):
        bi, i, _ = item(st, 0, 0)
        return bi, i, 0

    def xp_map(st):
        bi, i, _ = item(st, 0, 0)
        return bi, jnp.maximum(i * hb - 1, 0), 0

    def xx_map(st):
        bi, i, _ = item(st, 0, 0)
        return bi, jnp.minimum((i + 1) * hb, last_hb), 0

    def y_map(st):
        bi, i, _ = item(st, 0, 2)
        return bi, i, 0

    def w_in_spec(u, half):
        return pl.BlockSpec((1, d, tf), lambda st: (half * nf + item(st, u, 0)[2], 0, 0))

    def conv_spec(rows, u, half):
        return pl.BlockSpec((rows, tf), lambda st: (0, half * nf + item(st, u, 1)[2]))

    def w_out_spec(u):
        return pl.BlockSpec((tf, d), lambda st: (item(st, u, 2)[2], 0))

    w_specs = [w_in_spec(u, half) for u in (0, 1) for half in (0, 1)]
    w_args = [w_in_tiles] * 4
    for u in (0, 1):
        w_specs += [conv_spec(3, u, 0), conv_spec(3, u, 1),
                    conv_spec(1, u, 0), conv_spec(1, u, 1)]
        w_args += [conv_w, conv_w, conv_b, conv_b]
    w_specs += [w_out_spec(0), w_out_spec(1)]
    w_args += [w_out, w_out]

    kern = functools.partial(_ffn_kernel, t=t, n_tiles=n_tiles, nf=nf, n_steps=n_steps)
    return pl.pallas_call(
        kern,
        grid=(n_steps // 2 + 1,),
        in_specs=[
            pl.BlockSpec((1, halo, d), xp_map),
            pl.BlockSpec((1, t, d), x_map),
            pl.BlockSpec((1, halo, d), xx_map),
            pl.BlockSpec((1, d), lambda st: (0, 0)),
        ] + w_specs,
        out_specs=pl.BlockSpec((1, t, d), y_map),
        out_shape=jax.ShapeDtypeStruct((b, s, d), F32),
        scratch_shapes=[pltpu.VMEM((t + 2 * halo, d), BF16)]
        + [pltpu.VMEM((t, tf), BF16)] * 2
        + [pltpu.VMEM((t + 2 * halo, tf), F32)] * 4,
        compiler_params=_params(("arbitrary",)),
        name="ffn",
    )(x, x, x, g, *w_args)


def _final_norm_kernel(x_ref, g_ref, y_ref):
    y_ref[0] = _rms(x_ref[0], g_ref[...])


def _final_norm(x, g, *, t=1024):
    b, s, d = x.shape
    return pl.pallas_call(
        _final_norm_kernel,
        grid=(b, s // t),
        in_specs=[pl.BlockSpec((1, t, d), lambda bi, i: (bi, i, 0)),
                  pl.BlockSpec((1, d), lambda bi, i: (0, 0))],
        out_specs=pl.BlockSpec((1, t, d), lambda bi, i: (bi, i, 0)),
        out_shape=jax.ShapeDtypeStruct((b, s, d), F32),
        compiler_params=_params(("parallel", "parallel")),
        name="final_norm",
    )(x, g)


def _rope_tables(seq, dim):
    pos = jnp.arange(seq, dtype=F32)
    inv = 1.0 / (ROPE_THETA ** (jnp.arange(0, dim, 2, dtype=F32) / dim))
    ang = pos[:, None] * inv[None, :]
    return jnp.cos(ang), jnp.sin(ang)


def _prep_tables(seq):
    cos_a, sin_a = _rope_tables(seq, A_HEAD_DIM)
    cos2 = jnp.concatenate([cos_a, cos_a], axis=1)
    sin2 = jnp.concatenate([-sin_a, sin_a], axis=1)
    cos_b, sin_b = _rope_tables(seq, ROPE_DIM)
    half = ROPE_DIM // 2
    z = jnp.zeros((seq, half), F32)
    zz = jnp.zeros((seq, LANES - ROPE_DIM), F32)
    c = jnp.concatenate([cos_b, cos_b, zz], axis=1)
    s1 = jnp.concatenate([-sin_b, z, zz], axis=1)
    s2 = jnp.concatenate([z, sin_b, zz], axis=1)
    return (cos2, sin2), (c, s1, s2)


def _tile_cols_kernel(w_ref, o_ref):
    o_ref[0, 0] = w_ref[0].astype(BF16)


def _tile_cols(w, tn):
    layers, k, n = w.shape
    return pl.pallas_call(
        _tile_cols_kernel,
        grid=(layers, n // tn),
        in_specs=[pl.BlockSpec((1, k, tn), lambda l, j: (l, 0, j))],
        out_specs=pl.BlockSpec((1, 1, k, tn), lambda l, j: (l, j, 0, 0)),
        out_shape=jax.ShapeDtypeStruct((layers, n // tn, k, tn), BF16),
        compiler_params=_params(("parallel", "parallel")),
        name="tile_cols",
    )(w)


def _prep_b_weights(w_in, w_q_up, w_kv_up):
    pad = jnp.zeros((w_in.shape[0], LANES - ROPE_DIM), w_in.dtype)
    w_in_p = jnp.concatenate([w_in, pad], axis=1).astype(BF16)
    k = w_q_up.shape[0]
    wq = w_q_up.reshape(k, B_HEADS, NOPE_DIM + ROPE_DIM)
    wq = jnp.concatenate(
        [wq, jnp.zeros((k, B_HEADS, 2 * LANES - NOPE_DIM - ROPE_DIM), wq.dtype)], axis=2)
    wq = wq.reshape(k, B_HEADS * 2 * LANES).astype(BF16)
    wkv = w_kv_up.reshape(k, B_HEADS, NOPE_DIM + V_DIM)
    wk = wkv[:, :, :NOPE_DIM].reshape(k, B_HEADS * NOPE_DIM).astype(BF16)
    wvt = wkv[:, :, NOPE_DIM:].reshape(k, B_HEADS * V_DIM).T.astype(BF16)
    return w_in_p, wq, wk, wvt


def _trunk(x, p, tabs_a, tabs_b):
    depth = p["norm_mix"].shape[0]
    for i in range(depth):
        j = i // 2
        g_mix = p["norm_mix"][i][None, :]
        if i % 2 == 0:
            qkv = _qkv_a(x, g_mix, p["a_w_qkv"][j], *tabs_a)
            o = _win_attn(qkv, p["a_sink2"][j])
            x = _out_proj(o, p["a_w_o"][j], x)
        else:
            w_in_p, wq, wk, wvt = p["b_w"][j]
            cq, ckv, kr = _in_b(x, g_mix, w_in_p, p["b_q_norm"][j][None, :],
                                p["b_kv_norm"][j][None, :], *tabs_b)
            q = _proj_q(cq, wq, tabs_b)
            kn, vt = _proj_kv(ckv, wk, wvt)
            o = _mla_attn(q, kn, kr, vt)
            x = _out_proj(o, p["b_w_o"][j], x)
        x = _ffn(x, p["norm_ffn"][i][None, :], p["f_w_in"][i], p["f_conv_w"][i],
                 p["f_conv_b"][i][None, :], p["f_w_out"][i])
    return _final_norm(x, p["norm_final"][None, :])


def kernel(x_prompt, x_sample, norm_mix, norm_ffn, norm_final, a_w_qkv, a_w_o, a_sink,
           b_w_in, b_q_norm, b_w_q_up, b_kv_norm, b_w_kv_up, b_w_o,
           f_w_in, f_conv_w, f_conv_b, f_w_out):
    p = {
        "norm_mix": norm_mix, "norm_ffn": norm_ffn, "norm_final": norm_final,
        "a_w_qkv": a_w_qkv.astype(BF16), "a_w_o": a_w_o.astype(BF16),
        "a_sink2": a_sink * LOG2E,
        "b_q_norm": b_q_norm, "b_kv_norm": b_kv_norm, "b_w_o": b_w_o.astype(BF16),
        "b_w": [_prep_b_weights(b_w_in[j], b_w_q_up[j], b_w_kv_up[j])
                for j in range(b_w_in.shape[0])],
        "f_w_in": _tile_cols(f_w_in, FFN_TF), "f_conv_w": f_conv_w, "f_conv_b": f_conv_b,
        "f_w_out": f_w_out.astype(BF16),
    }
    outs = []
    for x in (x_prompt, x_sample):
        tabs_a, tabs_b = _prep_tables(x.shape[1])
        outs.append(_trunk(x, p, tabs_a, tabs_b))
    return tuple(outs)
```

```python
import functools
import math

import jax
import jax.numpy as jnp
from jax import lax
from jax.experimental import pallas as pl
from jax.experimental.pallas import tpu as pltpu

F32 = jnp.float32
BF16 = jnp.bfloat16

EPS = 1e-6
ROPE_THETA = 10000.0
LANES = 128
BF16_ROWS = 16
WINDOW = 128
A_HEADS = 16
A_KV_HEADS = 4
A_GROUP = A_HEADS // A_KV_HEADS
A_HEAD_DIM = 128
B_HEADS = 16
Q_LORA = 512
KV_LORA = 512
NOPE_DIM = 128
ROPE_DIM = 64
V_DIM = 128
LOG2E = math.log2(math.e)
NEG_BIG = -1e30
VMEM_LIMIT = 58 * 1024 * 1024


def _params(sem):
    return pltpu.CompilerParams(dimension_semantics=sem, vmem_limit_bytes=VMEM_LIMIT)


def _rms(x, g):
    return x * lax.rsqrt(jnp.mean(x * x, axis=-1, keepdims=True) + EPS) * g


def _rope64(x, c, s1, s2):
    return x * c + pltpu.roll(x, 96, 1) * s1 + pltpu.roll(x, 32, 1) * s2


def _qkv_a_kernel(x_ref, g_ref, w_ref, cos_ref, sin_ref, o_ref, xn_ref, *,
                  tn, q_cols, rope_cols, q_scale):
    xn_ref[...] = _rms(x_ref[0], g_ref[...]).astype(BF16)
    cos = cos_ref[...]
    sin = sin_ref[...]
    for j in range(w_ref.shape[1] // tn):
        acc = jnp.dot(xn_ref[...], w_ref[:, j * tn:(j + 1) * tn],
                      preferred_element_type=F32)
        for h in range(tn // LANES):
            col = j * tn + h * LANES
            ch = acc[:, h * LANES:(h + 1) * LANES]
            if col < rope_cols:
                ch = ch * cos + pltpu.roll(ch, LANES // 2, 1) * sin
            if col < q_cols:
                ch = ch * q_scale
            o_ref[0, :, col:col + LANES] = ch.astype(BF16)


def _qkv_a(x, g, w, cos2, sin2, *, t=512, tn=512):
    b, s, d = x.shape
    n = w.shape[1]
    q_cols = A_HEADS * A_HEAD_DIM
    k_cols = A_KV_HEADS * A_HEAD_DIM
    kern = functools.partial(
        _qkv_a_kernel, tn=tn, q_cols=q_cols, rope_cols=q_cols + k_cols,
        q_scale=A_HEAD_DIM ** -0.5 * LOG2E)
    return pl.pallas_call(
        kern,
        grid=(b, s // t),
        in_specs=[
            pl.BlockSpec((1, t, d), lambda bi, i: (bi, i, 0)),
            pl.BlockSpec((1, d), lambda bi, i: (0, 0)),
            pl.BlockSpec((d, n), lambda bi, i: (0, 0)),
            pl.BlockSpec((t, LANES), lambda bi, i: (i, 0)),
            pl.BlockSpec((t, LANES), lambda bi, i: (i, 0)),
        ],
        out_specs=pl.BlockSpec((1, t, n), lambda bi, i: (bi, i, 0)),
        out_shape=jax.ShapeDtypeStruct((b, s, n), BF16),
        scratch_shapes=[pltpu.VMEM((t, d), BF16)],
        compiler_params=_params(("parallel", "parallel")),
        name="qkv_a",
    )(x, g, w, cos2, sin2)


def _win_attn_kernel(sink_ref, q_ref, km_ref, kp_ref, kx_ref, vm_ref, vp_ref, vx_ref,
                     o_ref, kbuf, vbuf, *, tq, n_tiles):
    i = pl.program_id(1)
    blk = WINDOW
    span = 3 * blk
    kbuf[0:blk, :] = kp_ref[0]
    kbuf[blk:blk + tq, :] = km_ref[0]
    kbuf[blk + tq:, :] = kx_ref[0]
    vbuf[0:blk, :] = vp_ref[0]
    vbuf[blk:blk + tq, :] = vm_ref[0]
    vbuf[blk + tq:, :] = vx_ref[0]

    rows = A_GROUP * blk
    a_idx = lax.broadcasted_iota(jnp.int32, (rows, span), 0) % blk
    c_idx = lax.broadcasted_iota(jnp.int32, (rows, span), 1)
    rel = c_idx - a_idx
    band = (rel >= 0) & (rel <= 2 * WINDOW)
    grp = lax.broadcasted_iota(jnp.int32, (rows, 1), 0) // blk
    lo = jnp.where(i == 0, blk, 0)
    hi = jnp.where(i == n_tiles - 1, 2 * blk, span)

    n_sub = tq // blk
    for r in range(n_sub):
        valid = band
        if r == 0:
            valid = valid & (c_idx >= lo)
        if r == n_sub - 1:
            valid = valid & (c_idx < hi)
        for kh in range(A_KV_HEADS):
            qg = jnp.concatenate(
                [q_ref[0, r * blk:(r + 1) * blk,
                       (kh * A_GROUP + g) * A_HEAD_DIM:(kh * A_GROUP + g + 1) * A_HEAD_DIM]
                 for g in range(A_GROUP)], axis=0)
            kw = kbuf[r * blk:r * blk + span, kh * A_HEAD_DIM:(kh + 1) * A_HEAD_DIM]
            vw = vbuf[r * blk:r * blk + span, kh * A_HEAD_DIM:(kh + 1) * A_HEAD_DIM]
            s = lax.dot_general(qg, kw, (((1,), (1,)), ((), ())),
                                preferred_element_type=F32)
            s = jnp.where(valid, s, NEG_BIG)
            sk = jnp.zeros((rows, 1), F32)
            for g in range(A_GROUP):
                sk = jnp.where(grp == g, sink_ref[kh * A_GROUP + g], sk)
            m = jnp.maximum(jnp.max(s, axis=1, keepdims=True), sk)
            p = jnp.exp2(s - m)
            den = jnp.sum(p, axis=1, keepdims=True) + jnp.exp2(sk - m)
            o = jnp.dot(p.astype(BF16), vw, preferred_element_type=F32) / den
            for g in range(A_GROUP):
                col = (kh * A_GROUP + g) * A_HEAD_DIM
                o_ref[0, r * blk:(r + 1) * blk, col:col + A_HEAD_DIM] = (
                    o[g * blk:(g + 1) * blk].astype(BF16))


def _win_attn(qkv, sink2, *, tq=1024):
    b, s, _ = qkv.shape
    d = A_HEADS * A_HEAD_DIM
    kd = A_KV_HEADS * A_HEAD_DIM
    blk = WINDOW
    n_tiles = s // tq
    sub = tq // blk
    nb = s // blk
    k_col = d // kd
    v_col = k_col + 1

    def main(col):
        return pl.BlockSpec((1, tq, kd), lambda bi, i: (bi, i, col))

    def prev(col):
        return pl.BlockSpec((1, blk, kd), lambda bi, i: (bi, jnp.maximum(i * sub - 1, 0), col))

    def nxt(col):
        return pl.BlockSpec((1, blk, kd), lambda bi, i: (bi, jnp.minimum((i + 1) * sub, nb - 1), col))

    kern = functools.partial(_win_attn_kernel, tq=tq, n_tiles=n_tiles)
    return pl.pallas_call(
        kern,
        grid=(b, n_tiles),
        in_specs=[
            pl.BlockSpec(memory_space=pltpu.SMEM),
            pl.BlockSpec((1, tq, d), lambda bi, i: (bi, i, 0)),
            main(k_col), prev(k_col), nxt(k_col),
            main(v_col), prev(v_col), nxt(v_col),
        ],
        out_specs=pl.BlockSpec((1, tq, d), lambda bi, i: (bi, i, 0)),
        out_shape=jax.ShapeDtypeStruct((b, s, d), BF16),
        scratch_shapes=[pltpu.VMEM((tq + 2 * blk, kd), BF16),
                        pltpu.VMEM((tq + 2 * blk, kd), BF16)],
        compiler_params=_params(("parallel", "parallel")),
        name="win_attn",
    )(sink2, qkv, qkv, qkv, qkv, qkv, qkv, qkv)


def _in_b_kernel(x_ref, g_ref, w_ref, qn_ref, kvn_ref, c_ref, s1_ref, s2_ref,
                 cq_ref, ckv_ref, kr_ref):
    xn = _rms(x_ref[0], g_ref[...]).astype(BF16)
    hq = jnp.dot(xn, w_ref[:, :Q_LORA], preferred_element_type=F32)
    hkv = jnp.dot(xn, w_ref[:, Q_LORA:Q_LORA + KV_LORA], preferred_element_type=F32)
    cq_ref[0] = _rms(hq, qn_ref[...]).astype(BF16)
    hr = jnp.dot(xn, w_ref[:, Q_LORA + KV_LORA:], preferred_element_type=F32)
    ckv_ref[0] = _rms(hkv, kvn_ref[...]).astype(BF16)
    kr_ref[0] = _rope64(hr, c_ref[...], s1_ref[...], s2_ref[...]).astype(BF16)


def _in_b(x, g, w, qn, kvn, c, s1, s2, *, t=1024):
    b, s, d = x.shape
    n = w.shape[1]
    row = lambda bi, i: (bi, i, 0)
    fixed = lambda bi, i: (0, 0)
    tab = pl.BlockSpec((t, LANES), lambda bi, i: (i, 0))
    return pl.pallas_call(
        _in_b_kernel,
        grid=(b, s // t),
        in_specs=[
            pl.BlockSpec((1, t, d), row),
            pl.BlockSpec((1, d), fixed),
            pl.BlockSpec((d, n), fixed),
            pl.BlockSpec((1, Q_LORA), fixed),
            pl.BlockSpec((1, KV_LORA), fixed),
            tab, tab, tab,
        ],
        out_specs=[pl.BlockSpec((1, t, Q_LORA), row),
                   pl.BlockSpec((1, t, KV_LORA), row),
                   pl.BlockSpec((1, t, LANES), row)],
        out_shape=[jax.ShapeDtypeStruct((b, s, Q_LORA), BF16),
                   jax.ShapeDtypeStruct((b, s, KV_LORA), BF16),
                   jax.ShapeDtypeStruct((b, s, LANES), BF16)],
        compiler_params=_params(("parallel", "parallel")),
        name="in_b",
    )(x, g, w, qn, kvn, c, s1, s2)


def _proj_q_kernel(x_ref, w_ref, c_ref, s1_ref, s2_ref, o_ref, *, tn, scale):
    c = c_ref[...] * scale
    s1 = s1_ref[...] * scale
    s2 = s2_ref[...] * scale
    hw = 2 * LANES
    for j in range(w_ref.shape[1] // tn):
        acc = jnp.dot(x_ref[0], w_ref[:, j * tn:(j + 1) * tn], preferred_element_type=F32)
        for h in range(tn // hw):
            col = j * tn + h * hw
            o_ref[0, :, col:col + LANES] = (
                acc[:, h * hw:h * hw + LANES] * scale).astype(BF16)
            o_ref[0, :, col + LANES:col + hw] = _rope64(
                acc[:, h * hw + LANES:(h + 1) * hw], c, s1, s2).astype(BF16)


def _proj_q(x, w, tables, *, t=1024, tn=1024):
    b, s, k = x.shape
    n = w.shape[1]
    kern = functools.partial(_proj_q_kernel, tn=tn,
                             scale=(NOPE_DIM + ROPE_DIM) ** -0.5 * LOG2E)
    tab = pl.BlockSpec((t, LANES), lambda bi, i: (i, 0))
    return pl.pallas_call(
        kern,
        grid=(b, s // t),
        in_specs=[pl.BlockSpec((1, t, k), lambda bi, i: (bi, i, 0)),
                  pl.BlockSpec((k, n), lambda bi, i: (0, 0)),
                  tab, tab, tab],
        out_specs=pl.BlockSpec((1, t, n), lambda bi, i: (bi, i, 0)),
        out_shape=jax.ShapeDtypeStruct((b, s, n), BF16),
        compiler_params=_params(("parallel", "parallel")),
        name="proj_q",
    )(x, w, *tables)


def _proj_kv_kernel(x_ref, wk_ref, wvt_ref, kn_ref, vt_ref, *, tn):
    x = x_ref[0]
    for j in range(wk_ref.shape[1] // tn):
        cols = slice(j * tn, (j + 1) * tn)
        kn_ref[0, :, cols] = jnp.dot(x, wk_ref[:, cols],
                                     preferred_element_type=F32).astype(BF16)
        vt_ref[0, cols, :] = lax.dot_general(wvt_ref[cols, :], x, (((1,), (1,)), ((), ())),
                                             preferred_element_type=F32).astype(BF16)


def _proj_kv(x, wk, wvt, *, t=1024, tn=512):
    b, s, k = x.shape
    n = wk.shape[1]
    return pl.pallas_call(
        functools.partial(_proj_kv_kernel, tn=tn),
        grid=(b, s // t),
        in_specs=[pl.BlockSpec((1, t, k), lambda bi, i: (bi, i, 0)),
                  pl.BlockSpec((k, n), lambda bi, i: (0, 0)),
                  pl.BlockSpec((n, k), lambda bi, i: (0, 0))],
        out_specs=[pl.BlockSpec((1, t, n), lambda bi, i: (bi, i, 0)),
                   pl.BlockSpec((1, n, t), lambda bi, i: (bi, 0, i))],
        out_shape=[jax.ShapeDtypeStruct((b, s, n), BF16),
                   jax.ShapeDtypeStruct((b, n, s), BF16)],
        compiler_params=_params(("parallel", "parallel")),
        name="proj_kv",
    )(x, wk, wvt)


def _mla_kernel(q_ref, kn_ref, kr_ref, vt_ref, o_ref, kcat_ref, st_ref, m_ref, *,
                n_i, kc, n_steps):
    t = pl.program_id(0)
    s_len, tq = st_ref.shape
    sub8 = 8

    @pl.when(t == 0)
    def _():
        st_ref[...] = jnp.zeros(st_ref.shape, F32)
        m_ref[...] = jnp.zeros(m_ref.shape, F32)

    @pl.when((t % n_i == 0) & (t < n_steps))
    def _():
        kcat_ref[:, :LANES] = kn_ref[0]
        kcat_ref[:, LANES:] = kr_ref[0]

    q = q_ref[0]
    m_old = m_ref[...]
    m_new = None
    den8 = None
    ot = None
    for c in range(s_len // kc):
        rows = slice(c * kc, (c + 1) * kc)
        p = jnp.exp2(st_ref[rows, :] - m_old)
        part = jnp.sum(p.reshape(kc // sub8, sub8, tq), axis=0)
        den8 = part if den8 is None else den8 + part
        pv = jnp.dot(vt_ref[0, :, rows], p.astype(BF16), preferred_element_type=F32)
        ot = pv if ot is None else ot + pv
        s_new = lax.dot_general(kcat_ref[rows, :], q, (((1,), (1,)), ((), ())),
                                preferred_element_type=F32)
        st_ref[rows, :] = s_new
        part = jnp.max(s_new.reshape(kc // sub8, sub8, tq), axis=0)
        m_new = part if m_new is None else jnp.maximum(m_new, part)
    m_ref[...] = jnp.max(m_new, axis=0, keepdims=True)
    den = jnp.sum(den8, axis=0, keepdims=True)
    o_ref[0] = (ot / den).T.astype(BF16)


def _mla_attn(q, kn, kr, vt, *, tq=512, kc=512):
    b, s, _ = q.shape
    qw = 2 * LANES
    n_i = s // tq
    n_steps = b * B_HEADS * n_i

    def item(t, lag):
        t = jnp.clip(t - lag, 0, n_steps - 1)
        bh = t // n_i
        return bh // B_HEADS, bh % B_HEADS, t % n_i

    def q_map(t):
        bi, h, i = item(t, 0)
        return bi, i, h

    def kn_map(t):
        bi, h, _ = item(t, 0)
        return bi, 0, h

    def kr_map(t):
        bi, _, _ = item(t, 0)
        return bi, 0, 0

    def vt_map(t):
        bi, h, _ = item(t, 1)
        return bi, h, 0

    def o_map(t):
        bi, h, i = item(t, 1)
        return bi, i, h

    kern = functools.partial(_mla_kernel, n_i=n_i, kc=kc, n_steps=n_steps)
    return pl.pallas_call(
        kern,
        grid=(n_steps + 1,),
        in_specs=[
            pl.BlockSpec((1, tq, qw), q_map),
            pl.BlockSpec((1, s, NOPE_DIM), kn_map),
            pl.BlockSpec((1, s, LANES), kr_map),
            pl.BlockSpec((1, V_DIM, s), vt_map),
        ],
        out_specs=pl.BlockSpec((1, tq, V_DIM), o_map),
        out_shape=jax.ShapeDtypeStruct((b, s, B_HEADS * V_DIM), BF16),
        scratch_shapes=[pltpu.VMEM((s, qw), BF16),
                        pltpu.VMEM((s, tq), F32),
                        pltpu.VMEM((1, tq), F32)],
        compiler_params=_params(("arbitrary",)),
        name="mla_attn",
    )(q, kn, kr, vt)


def _out_proj_kernel(o_ref, w_ref, x_ref, y_ref, *, tn):
    for j in range(w_ref.shape[1] // tn):
        cols = slice(j * tn, (j + 1) * tn)
        y_ref[0, :, cols] = x_ref[0, :, cols] + jnp.dot(
            o_ref[0], w_ref[:, cols], preferred_element_type=F32)


def _out_proj(o, w, x, *, t=512, tn=512):
    b, s, k = o.shape
    n = w.shape[1]
    return pl.pallas_call(
        functools.partial(_out_proj_kernel, tn=tn),
        grid=(b, s // t),
        in_specs=[pl.BlockSpec((1, t, k), lambda bi, i: (bi, i, 0)),
                  pl.BlockSpec((k, n), lambda bi, i: (0, 0)),
                  pl.BlockSpec((1, t, n), lambda bi, i: (bi, i, 0))],
        out_specs=pl.BlockSpec((1, t, n), lambda bi, i: (bi, i, 0)),
        out_shape=jax.ShapeDtypeStruct((b, s, n), F32),
        compiler_params=_params(("parallel", "parallel")),
        name="out_proj",
    )(o, w, x)


def _ffn_kernel(xp_ref, x_ref, xx_ref, g_ref, *rest, t, n_tiles, nf, n_steps):
    w_refs, (y_ref, xn_ref, a0_ref, a1_ref, hg0_ref, hg1_ref, hu0_ref, hu1_ref) = (
        rest[:14], rest[14:])
    up_w = (w_refs[0:2], w_refs[2:4])
    conv_w = (w_refs[4:8], w_refs[8:12])
    down_w = w_refs[12:14]
    k = 2 * pl.program_id(0)
    f = k % nf
    i = (k // nf) % n_tiles
    halo = BF16_ROWS

    a_refs = (a0_ref, a1_ref)
    hg_refs = (hg0_ref, hg1_ref)
    hu_refs = (hu0_ref, hu1_ref)

    @pl.when(k == 0)
    def _():
        for ref in a_refs + hg_refs + hu_refs + (y_ref,):
            ref[...] = jnp.zeros(ref.shape, ref.dtype)

    @pl.when((f == 0) & (k < n_steps))
    def _():
        g = g_ref[...]
        prev = jnp.where(i > 0, _rms(xp_ref[0], g), 0.0)
        nxt = jnp.where(i < n_tiles - 1, _rms(xx_ref[0], g), 0.0)
        xn_ref[0:halo, :] = prev.astype(BF16)
        xn_ref[halo:halo + t, :] = _rms(x_ref[0], g).astype(BF16)
        xn_ref[halo + t:, :] = nxt.astype(BF16)

    @pl.when(f == 2)
    def _():
        y_ref[0] = x_ref[0]

    tf = a0_ref.shape[1]
    d = y_ref.shape[2]
    rb = 128
    blocks = [(r, c) for c in range(tf // LANES) for r in range(t // rb)]
    m_rows = t + 2 * halo
    up_rows = m_rows // (3 if m_rows % (3 * BF16_ROWS) == 0 else 2)
    down_rows = min(t, 512)
    down_cols = 512

    def conv(h_ref, r, cols, cw_ref, cb_ref):
        cw = cw_ref[:, cols]
        lo = halo + r * rb
        blk = h_ref[lo - 8:lo + rb + 8, cols]
        return (blk[7:7 + rb] * cw[0:1] + blk[8:8 + rb] * cw[1:2]
                + blk[9:9 + rb] * cw[2:3] + cb_ref[:, cols])

    def stages(p):
        wg_ref, wu_ref = up_w[p]
        cwg_ref, cwu_ref, cbg_ref, cbu_ref = conv_w[p]
        wo_ref = down_w[p]

        def gate_block(r, c):
            cols = slice(c * LANES, (c + 1) * LANES)
            gc = conv(hg_refs[1 - p], r, cols, cwg_ref, cbg_ref)
            uc = conv(hu_refs[1 - p], r, cols, cwu_ref, cbu_ref)
            a_refs[1 - p][r * rb:(r + 1) * rb, cols] = (
                gc * jax.nn.sigmoid(gc) * uc).astype(BF16)

        def up_piece(h_ref, w_ref, lo):
            h_ref[lo:lo + up_rows, :] = jnp.dot(xn_ref[lo:lo + up_rows, :], w_ref[0],
                                                preferred_element_type=F32)

        def down_piece(r0, c0):
            rows = slice(r0, r0 + down_rows)
            cols = slice(c0, c0 + down_cols)
            y_ref[0, rows, cols] += jnp.dot(a_refs[p][rows, :], wo_ref[:, cols],
                                            preferred_element_type=F32)

        pieces = [(functools.partial(up_piece, h_ref, w_ref, lo), up_rows * d)
                  for lo in range(0, m_rows, up_rows)
                  for h_ref, w_ref in ((hg_refs[p], wg_ref), (hu_refs[p], wu_ref))]
        pieces += [(functools.partial(down_piece, r0, c0), down_rows * down_cols * 2)
                   for r0 in range(0, t, down_rows) for c0 in range(0, d, down_cols)]
        total = sum(w for _, w in pieces)
        done = 0
        acc = 0
        for piece, w in pieces:
            piece()
            acc += w
            upto = len(blocks) * acc // total
            for r, c in blocks[done:upto]:
                gate_block(r, c)
            done = upto

    stages(0)

    @pl.when(k >= 0)
    def _():
        stages(1)


FFN_TF = 256


def _ffn(x, g, w_in_tiles, conv_w, conv_b, w_out, *, t=1024):
    b, s, d = x.shape
    d_ff = w_out.shape[0]
    tf = w_in_tiles.shape[2]
    nf = d_ff // tf
    assert nf >= 4 and nf % 2 == 0 and w_in_tiles.shape[0] == 2 * nf
    n_tiles = s // t
    n_steps = b * n_tiles * nf
    halo = BF16_ROWS
    hb = t // halo
    last_hb = s // halo - 1

    def item(st, u, lag):
        k = jnp.clip(2 * st + u - lag, 0, n_steps - 1)
        tile = k // nf
        return tile // n_tiles, tile % n_tiles, k % nf

    def x_map(st):
        bi, i, _ = item(st, 0, 0)
        return bi, i, 0

    def xp_map(st):
        bi, i, _ = item(st, 0, 0)
        return bi, jnp.maximum(i * hb - 1, 0), 0

    def xx_map(st):
        bi, i, _ = item(st, 0, 0)
        return bi, jnp.minimum((i + 1) * hb, last_hb), 0

    def y_map(st):
        bi, i, _ = item(st, 0, 2)
        return bi, i, 0

    def w_in_spec(u, half):
        return pl.BlockSpec((1, d, tf), lambda st: (half * nf + item(st, u, 0)[2], 0, 0))

    def conv_spec(rows, u, half):
        return pl.BlockSpec((rows, tf), lambda st: (0, half * nf + item(st, u, 1)[2]))

    def w_out_spec(u):
        return pl.BlockSpec((tf, d), lambda st: (item(st, u, 2)[2], 0))

    w_specs = [w_in_spec(u, half) for u in (0, 1) for half in (0, 1)]
    w_args = [w_in_tiles] * 4
    for u in (0, 1):
        w_specs += [conv_spec(3, u, 0), conv_spec(3, u, 1),
                    conv_spec(1, u, 0), conv_spec(1, u, 1)]
        w_args += [conv_w, conv_w, conv_b, conv_b]
    w_specs += [w_out_spec(0), w_out_spec(1)]
    w_args += [w_out, w_out]

    kern = functools.partial(_ffn_kernel, t=t, n_tiles=n_tiles, nf=nf, n_steps=n_steps)
    return pl.pallas_call(
        kern,
        grid=(n_steps // 2 + 1,),
        in_specs=[
            pl.BlockSpec((1, halo, d), xp_map),
            pl.BlockSpec((1, t, d), x_map),
            pl.BlockSpec((1, halo, d), xx_map),
            pl.BlockSpec((1, d), lambda st: (0, 0)),
        ] + w_specs,
        out_specs=pl.BlockSpec((1, t, d), y_map),
        out_shape=jax.ShapeDtypeStruct((b, s, d), F32),
        scratch_shapes=[pltpu.VMEM((t + 2 * halo, d), BF16)]
        + [pltpu.VMEM((t, tf), BF16)] * 2
        + [pltpu.VMEM((t + 2 * halo, tf), F32)] * 4,
        compiler_params=_params(("arbitrary",)),
        name="ffn",
    )(x, x, x, g, *w_args)


def _final_norm_kernel(x_ref, g_ref, y_ref):
    y_ref[0] = _rms(x_ref[0], g_ref[...])


def _final_norm(x, g, *, t=1024):
    b, s, d = x.shape
    return pl.pallas_call(
        _final_norm_kernel,
        grid=(b, s // t),
        in_specs=[pl.BlockSpec((1, t, d), lambda bi, i: (bi, i, 0)),
                  pl.BlockSpec((1, d), lambda bi, i: (0, 0))],
        out_specs=pl.BlockSpec((1, t, d), lambda bi, i: (bi, i, 0)),
        out_shape=jax.ShapeDtypeStruct((b, s, d), F32),
        compiler_params=_params(("parallel", "parallel")),
        name="final_norm",
    )(x, g)


def _rope_tables(seq, dim):
    pos = jnp.arange(seq, dtype=F32)
    inv = 1.0 / (ROPE_THETA ** (jnp.arange(0, dim, 2, dtype=F32) / dim))
    ang = pos[:, None] * inv[None, :]
    return jnp.cos(ang), jnp.sin(ang)


def _prep_tables(seq):
    cos_a, sin_a = _rope_tables(seq, A_HEAD_DIM)
    cos2 = jnp.concatenate([cos_a, cos_a], axis=1)
    sin2 = jnp.concatenate([-sin_a, sin_a], axis=1)
    cos_b, sin_b = _rope_tables(seq, ROPE_DIM)
    half = ROPE_DIM // 2
    z = jnp.zeros((seq, half), F32)
    zz = jnp.zeros((seq, LANES - ROPE_DIM), F32)
    c = jnp.concatenate([cos_b, cos_b, zz], axis=1)
    s1 = jnp.concatenate([-sin_b, z, zz], axis=1)
    s2 = jnp.concatenate([z, sin_b, zz], axis=1)
    return (cos2, sin2), (c, s1, s2)


def _tile_cols_kernel(w_ref, o_ref):
    o_ref[0, 0] = w_ref[0].astype(BF16)


def _tile_cols(w, tn):
    layers, k, n = w.shape
    return pl.pallas_call(
        _tile_cols_kernel,
        grid=(layers, n // tn),
        in_specs=[pl.BlockSpec((1, k, tn), lambda l, j: (l, 0, j))],
        out_specs=pl.BlockSpec((1, 1, k, tn), lambda l, j: (l, j, 0, 0)),
        out_shape=jax.ShapeDtypeStruct((layers, n // tn, k, tn), BF16),
        compiler_params=_params(("parallel", "parallel")),
        name="tile_cols",
    )(w)


def _prep_b_weights(w_in, w_q_up, w_kv_up):
    pad = jnp.zeros((w_in.shape[0], LANES - ROPE_DIM), w_in.dtype)
    w_in_p = jnp.concatenate([w_in, pad], axis=1).astype(BF16)
    k = w_q_up.shape[0]
    wq = w_q_up.reshape(k, B_HEADS, NOPE_DIM + ROPE_DIM)
    wq = jnp.concatenate(
        [wq, jnp.zeros((k, B_HEADS, 2 * LANES - NOPE_DIM - ROPE_DIM), wq.dtype)], axis=2)
    wq = wq.reshape(k, B_HEADS * 2 * LANES).astype(BF16)
    wkv = w_kv_up.reshape(k, B_HEADS, NOPE_DIM + V_DIM)
    wk = wkv[:, :, :NOPE_DIM].reshape(k, B_HEADS * NOPE_DIM).astype(BF16)
    wvt = wkv[:, :, NOPE_DIM:].reshape(k, B_HEADS * V_DIM).T.astype(BF16)
    return w_in_p, wq, wk, wvt


def _trunk(x, p, tabs_a, tabs_b):
    depth = p["norm_mix"].shape[0]
    for i in range(depth):
        j = i // 2
        g_mix = p["norm_mix"][i][None, :]
        if i % 2 == 0:
            qkv = _qkv_a(x, g_mix, p["a_w_qkv"][j], *tabs_a)
            o = _win_attn(qkv, p["a_sink2"][j])
            x = _out_proj(o, p["a_w_o"][j], x)
        else:
            w_in_p, wq, wk, wvt = p["b_w"][j]
            cq, ckv, kr = _in_b(x, g_mix, w_in_p, p["b_q_norm"][j][None, :],
                                p["b_kv_norm"][j][None, :], *tabs_b)
            q = _proj_q(cq, wq, tabs_b)
            kn, vt = _proj_kv(ckv, wk, wvt)
            o = _mla_attn(q, kn, kr, vt)
            x = _out_proj(o, p["b_w_o"][j], x)
        x = _ffn(x, p["norm_ffn"][i][None, :], p["f_w_in"][i], p["f_conv_w"][i],
                 p["f_conv_b"][i][None, :], p["f_w_out"][i])
    return _final_norm(x, p["norm_final"][None, :])


def kernel(x_prompt, x_sample, norm_mix, norm_ffn, norm_final, a_w_qkv, a_w_o, a_sink,
           b_w_in, b_q_norm, b_w_q_up, b_kv_norm, b_w_kv_up, b_w_o,
           f_w_in, f_conv_w, f_conv_b, f_w_out):
    p = {
        "norm_mix": norm_mix, "norm_ffn": norm_ffn, "norm_final": norm_final,
        "a_w_qkv": a_w_qkv.astype(BF16), "a_w_o": a_w_o.astype(BF16),
        "a_sink2": a_sink * LOG2E,
        "b_q_norm": b_q_norm, "b_kv_norm": b_kv_norm, "b_w_o": b_w_o.astype(BF16),
        "b_w": [_prep_b_weights(b_w_in[j], b_w_q_up[j], b_w_kv_up[j])
                for j in range(b_w_in.shape[0])],
        "f_w_in": _tile_cols(f_w_in, FFN_TF), "f_conv_w": f_conv_w, "f_conv_b": f_conv_b,
        "f_w_out": f_w_out.astype(BF16),
    }
    outs = []
    for x in (x_prompt, x_sample):
        tabs_a, tabs_b = _prep_tables(x.shape[1])
        outs.append(_trunk(x, p, tabs_a, tabs_b))
    return tuple(outs)
```
